```python
import math
import jax, jax.numpy as jnp
from jax import lax
import numpy as np

D_MODEL = 4096
BATCH = 2
SEQ = 4096
DEPTH = 1
DEC_BATCH = 16
DEC_SEQ = 64
PAST_LEN = 1024

CHUNK = 64
Q_BLOCK = 128
CONV_DIM = D_MODEL
CONV_WIDTH = 31
N_HEADS = 64
Q_RANK = 1024
KV_RANK = 512
NOPE_DIM = 128
ROPE_DIM = 64
V_DIM = 128
QK_DIM = NOPE_DIM + ROPE_DIM
ATTN_DIM = N_HEADS * V_DIM
D_FF = 4 * D_MODEL
ROPE_THETA = 10000.0
ALPHA = (2 * DEPTH) ** 0.25
BETA = (8 * DEPTH) ** -0.25
LN_EPS = 1e-5
RMS_EPS = 1e-6
NEG_INF = -1e30
IN_SIZES = (CONV_DIM, CONV_DIM, Q_RANK, KV_RANK + ROPE_DIM, D_MODEL, D_MODEL)
IN_DIM = sum(IN_SIZES)
IN_SPLIT_POINTS = [int(v) for v in np.cumsum(IN_SIZES)[:-1]]

kernel_name = "streaming_conformer_mla_hybrid_step"


def layer_norm(x, g, b):
    xf = x.astype(jnp.float32)
    mu = xf.mean(-1, keepdims=True)
    var = jnp.square(xf - mu).mean(-1, keepdims=True)
    return ((xf - mu) * lax.rsqrt(var + LN_EPS)).astype(x.dtype) * g + b


def rms_norm(x, g):
    xf = x.astype(jnp.float32)
    return (xf * lax.rsqrt(jnp.square(xf).mean(-1, keepdims=True) + RMS_EPS)).astype(x.dtype) * g


def rope(x, pos):
    half = ROPE_DIM // 2
    inv = ROPE_THETA ** (-jnp.arange(half, dtype=jnp.float32) / half)
    ang = pos.astype(jnp.float32)[:, None] * inv[None, :]
    shape = (1, pos.shape[0]) + (1,) * (x.ndim - 3) + (half,)
    cos = jnp.cos(ang).reshape(shape).astype(x.dtype)
    sin = jnp.sin(ang).reshape(shape).astype(x.dtype)
    x1, x2 = x[..., :half], x[..., half:]
    return jnp.concatenate([x1 * cos - x2 * sin, x1 * sin + x2 * cos], axis=-1)


def in_project(x, w_in, b_in):
    return jnp.split(x @ w_in + b_in, IN_SPLIT_POINTS, axis=-1)


def conv_module(glu_a, glu_b, hist, w_dw, b_dw, cn_g, cn_b, w_pw):
    u = glu_a * jax.nn.sigmoid(glu_b)
    full = jnp.concatenate([hist.astype(u.dtype), u], axis=1)
    y = lax.conv_general_dilated(full, w_dw[:, None, :], window_strides=(1,), padding='VALID',
                                 dimension_numbers=('NWC', 'WIO', 'NWC'),
                                 feature_group_count=CONV_DIM) + b_dw
    y = jax.nn.silu(layer_norm(y, cn_g, cn_b))
    return y @ w_pw, full[:, -(CONV_WIDTH - 1):]


def mla_latents(q_lat, kv_lat, pos, q_a_g, w_q_b, kv_a_g):
    b, t, _ = q_lat.shape
    q = (rms_norm(q_lat, q_a_g) @ w_q_b).reshape(b, t, N_HEADS, QK_DIM)
    q_nope = q[..., :NOPE_DIM]
    q_rope = rope(q[..., NOPE_DIM:], pos)
    ckv = rms_norm(kv_lat[..., :KV_RANK], kv_a_g)
    k_rope = rope(kv_lat[..., KV_RANK:], pos)
    return q_nope, q_rope, ckv, k_rope


def mla_prompt(q_nope, q_rope, ckv, k_rope, w_kv_b, w_o):
    b, s = q_nope.shape[0], q_nope.shape[1]
    kv = (ckv @ w_kv_b).reshape(b, s, N_HEADS, NOPE_DIM + V_DIM)
    k = jnp.concatenate([kv[..., :NOPE_DIM],
                         jnp.broadcast_to(k_rope[:, :, None, :], (b, s, N_HEADS, ROPE_DIM))], axis=-1)
    v = kv[..., NOPE_DIM:]
    q = jnp.concatenate([q_nope, q_rope], axis=-1)
    nb = s // Q_BLOCK
    q_blocks = q.reshape(b, nb, Q_BLOCK, N_HEADS, QK_DIM).transpose(1, 0, 2, 3, 4)
    key_chunk = jnp.arange(s) // CHUNK
    scale = QK_DIM ** -0.5

    def block(args):
        qb, i = args
        q_chunk = (i * Q_BLOCK + jnp.arange(Q_BLOCK)) // CHUNK
        sc = jnp.einsum('bqhd,bkhd->bhqk', qb, k).astype(jnp.float32) * scale
        sc = jnp.where(key_chunk[None, :] <= q_chunk[:, None], sc, NEG_INF)
        p = jax.nn.softmax(sc, axis=-1).astype(v.dtype)
        return jnp.einsum('bhqk,bkhd->bqhd', p, v)

    o = lax.map(block, (q_blocks, jnp.arange(nb)))
    o = o.transpose(1, 0, 2, 3, 4).reshape(b, s, ATTN_DIM)
    return o @ w_o


def mla_sample(q_nope, q_rope, ckv_all, krope_all, w_kv_b, w_o):
    b, t = q_nope.shape[0], q_nope.shape[1]
    w = w_kv_b.reshape(KV_RANK, N_HEADS, NOPE_DIM + V_DIM)
    w_uk, w_uv = w[..., :NOPE_DIM], w[..., NOPE_DIM:]
    q_abs = jnp.einsum('bqhd,rhd->bqhr', q_nope, w_uk)
    sc = (jnp.einsum('bqhr,bkr->bhqk', q_abs, ckv_all)
          + jnp.einsum('bqhd,bkd->bhqk', q_rope, krope_all)).astype(jnp.float32) * (QK_DIM ** -0.5)
    p = jax.nn.softmax(sc, axis=-1).astype(ckv_all.dtype)
    o_lat = jnp.einsum('bhqk,bkr->bqhr', p, ckv_all)
    o = jnp.einsum('bqhr,rhd->bqhd', o_lat, w_uv).reshape(b, t, ATTN_DIM)
    return o @ w_o


def merge_and_ffn(x, conv_out, attn_out, gate_c, gate_a, w_out, ln1_g, ln1_b, w_up, w_down, ln2_g, ln2_b):
    merged = jax.nn.sigmoid(gate_c) * conv_out + jax.nn.sigmoid(gate_a) * attn_out
    h = layer_norm(ALPHA * x + merged @ w_out, ln1_g, ln1_b)
    f = jnp.square(jax.nn.relu(h @ w_up)) @ w_down
    return layer_norm(ALPHA * h + f, ln2_g, ln2_b)


def setup_inputs(seed: int = 0) -> dict:
    key = jax.random.key(seed)
    ks = jax.random.split(key, 32)

    def nrm(k, shape, scale):
        return jax.random.normal(k, shape, jnp.float32) * scale

    L = DEPTH
    return {
        "x_prompt": nrm(ks[0], (BATCH, SEQ, D_MODEL), 1.0),
        "x_sample": nrm(ks[1], (DEC_BATCH, DEC_SEQ, D_MODEL), 1.0),
        "cache_ckv": nrm(ks[2], (L, DEC_BATCH, PAST_LEN, KV_RANK), 1.0),
        "cache_krope": nrm(ks[3], (L, DEC_BATCH, PAST_LEN, ROPE_DIM), 1.0),
        "state_conv": nrm(ks[4], (L, DEC_BATCH, CONV_WIDTH - 1, CONV_DIM), 0.5),
        "w_in": nrm(ks[5], (L, D_MODEL, IN_DIM), D_MODEL ** -0.5),
        "b_in": nrm(ks[6], (L, IN_DIM), 0.01),
        "w_dw": nrm(ks[7], (L, CONV_WIDTH, CONV_DIM), CONV_WIDTH ** -0.5),
        "b_dw": nrm(ks[8], (L, CONV_DIM), 0.01),
        "conv_ln_g": 1.0 + nrm(ks[9], (L, CONV_DIM), 0.01),
        "conv_ln_b": nrm(ks[10], (L, CONV_DIM), 0.01),
        "w_conv_pw": nrm(ks[11], (L, CONV_DIM, D_MODEL), CONV_DIM ** -0.5),
        "q_a_g": 1.0 + nrm(ks[12], (L, Q_RANK), 0.01),
        "w_q_b": nrm(ks[13], (L, Q_RANK, N_HEADS * QK_DIM), Q_RANK ** -0.5),
        "kv_a_g": 1.0 + nrm(ks[14], (L, KV_RANK), 0.01),
        "w_kv_b": nrm(ks[15], (L, KV_RANK, N_HEADS * (NOPE_DIM + V_DIM)), KV_RANK ** -0.5),
        "w_attn_o": nrm(ks[16], (L, ATTN_DIM, D_MODEL), ATTN_DIM ** -0.5),
        "w_out": nrm(ks[17], (L, D_MODEL, D_MODEL), BETA * D_MODEL ** -0.5),
        "ln1_g": 1.0 + nrm(ks[18], (L, D_MODEL), 0.01),
        "ln1_b": nrm(ks[19], (L, D_MODEL), 0.01),
        "w_up": nrm(ks[20], (L, D_MODEL, D_FF), D_MODEL ** -0.5),
        "w_down": nrm(ks[21], (L, D_FF, D_MODEL), BETA * D_FF ** -0.5),
        "ln2_g": 1.0 + nrm(ks[22], (L, D_MODEL), 0.01),
        "ln2_b": nrm(ks[23], (L, D_MODEL), 0.01),
    }


def reference(x_prompt, x_sample, cache_ckv, cache_krope, state_conv, w_in, b_in, w_dw, b_dw,
              conv_ln_g, conv_ln_b, w_conv_pw, q_a_g, w_q_b, kv_a_g, w_kv_b, w_attn_o, w_out,
              ln1_g, ln1_b, w_up, w_down, ln2_g, ln2_b):
    pos_p = jnp.arange(x_prompt.shape[1])
    pos_s = PAST_LEN + jnp.arange(x_sample.shape[1])
    hp, hs = x_prompt, x_sample
    ckv_p_l, kr_p_l, cs_p_l, ckv_s_l, kr_s_l, cs_s_l = [], [], [], [], [], []
    for l in range(DEPTH):
        ga, gb, ql, kl, gc, gat = in_project(hp, w_in[l], b_in[l])
        zero_hist = jnp.zeros((hp.shape[0], CONV_WIDTH - 1, CONV_DIM), hp.dtype)
        conv_p, cs_p = conv_module(ga, gb, zero_hist, w_dw[l], b_dw[l], conv_ln_g[l], conv_ln_b[l], w_conv_pw[l])
        qn, qr, ckv_p, kr_p = mla_latents(ql, kl, pos_p, q_a_g[l], w_q_b[l], kv_a_g[l])
        attn_p = mla_prompt(qn, qr, ckv_p, kr_p, w_kv_b[l], w_attn_o[l])
        hp = merge_and_ffn(hp, conv_p, attn_p, gc, gat, w_out[l], ln1_g[l], ln1_b[l],
                           w_up[l], w_down[l], ln2_g[l], ln2_b[l])
        ga, gb, ql, kl, gc, gat = in_project(hs, w_in[l], b_in[l])
        conv_s, cs_s = conv_module(ga, gb, state_conv[l], w_dw[l], b_dw[l], conv_ln_g[l], conv_ln_b[l], w_conv_pw[l])
        qn, qr, ckv_s, kr_s = mla_latents(ql, kl, pos_s, q_a_g[l], w_q_b[l], kv_a_g[l])
        ckv_all = jnp.concatenate([cache_ckv[l].astype(ckv_s.dtype), ckv_s], axis=1)
        kr_all = jnp.concatenate([cache_krope[l].astype(kr_s.dtype), kr_s], axis=1)
        attn_s = mla_sample(qn, qr, ckv_all, kr_all, w_kv_b[l], w_attn_o[l])
        hs = merge_and_ffn(hs, conv_s, attn_s, gc, gat, w_out[l], ln1_g[l], ln1_b[l],
                           w_up[l], w_down[l], ln2_g[l], ln2_b[l])
        ckv_p_l.append(ckv_p); kr_p_l.append(kr_p); cs_p_l.append(cs_p)
        ckv_s_l.append(ckv_s); kr_s_l.append(kr_s); cs_s_l.append(cs_s)
    return (hp, hs, jnp.stack(ckv_p_l), jnp.stack(kr_p_l), jnp.stack(cs_p_l),
            jnp.stack(ckv_s_l), jnp.stack(kr_s_l), jnp.stack(cs_s_l))
```

```python
import functools

import jax
import jax.numpy as jnp
from jax import lax
from jax.experimental import pallas as pl
from jax.experimental.pallas import tpu as pltpu

F32 = jnp.float32
BF16 = jnp.bfloat16

CHUNK = 64
CONV_WIDTH = 31
N_HEADS = 64
NOPE_DIM = 128
ROPE_DIM = 64
V_DIM = 128
QK_DIM = NOPE_DIM + ROPE_DIM
ROPE_THETA = 10000.0
LN_EPS = 1e-5
RMS_EPS = 1e-6
NEG_INF = -1e30

LANES = 128
SUBLANES = 8
VMEM_LIMIT_BYTES = 60 * 1024 * 1024

ROPE_SLOT = LANES
HEAD_SLOT = NOPE_DIM + ROPE_SLOT
HIST_ROWS = 32
HIST_PAD = HIST_ROWS - (CONV_WIDTH - 1)
CONV_ROWS = 64


def _tile(dim, pref):
    t = min(dim, pref)
    assert dim % t == 0, (dim, pref)
    return t


def _params(sem):
    return pltpu.CompilerParams(dimension_semantics=sem, vmem_limit_bytes=VMEM_LIMIT_BYTES)


def _mm_body(*refs, nw, ne, no, nk, epilogue):
    x_ref = refs[0]
    w_refs = refs[1:1 + nw]
    ex = refs[1 + nw:1 + nw + ne]
    outs = refs[1 + nw + ne:1 + nw + ne + no]
    accs = refs[1 + nw + ne + no:]
    parts = [jnp.dot(x_ref[...], w[...], preferred_element_type=F32) for w in w_refs]
    if nk == 1:
        epilogue(parts, ex, outs)
        return
    k = pl.program_id(2)

    @pl.when(k == 0)
    def _():
        for a, p in zip(accs, parts):
            a[...] = p

    @pl.when(jnp.logical_and(k > 0, k < nk - 1))
    def _():
        for a, p in zip(accs, parts):
            a[...] += p

    @pl.when(k == nk - 1)
    def _():
        epilogue([a[...] + p for a, p in zip(accs, parts)], ex, outs)


def _extra_spec(kind, arr, tm, tn, n):
    if kind == "row":
        return pl.BlockSpec((1, tn), lambda i, j, k: (0, j))
    if kind == "rows":
        return pl.BlockSpec((tm, arr.shape[1]), lambda i, j, k: (i, 0))
    off = kind * (n // tn)
    return pl.BlockSpec((tm, tn), lambda i, j, k: (i, j + off))


def _matmul(x, ws, extras, outs, epilogue, *, tm=1024, tn=1024, tk=4096):
    m, kdim = x.shape
    n = ws[0].shape[1]
    tm, tn, tk = _tile(m, tm), _tile(n, tn), _tile(kdim, tk)
    nk = kdim // tk
    in_specs = [pl.BlockSpec((tm, tk), lambda i, j, k: (i, k))]
    in_specs += [pl.BlockSpec((tk, tn), lambda i, j, k: (k, j)) for _ in ws]
    in_specs += [_extra_spec(kind, arr, tm, tn, n) for kind, arr in extras]
    out_specs = [pl.BlockSpec((tm, tn), lambda i, j, k: (i, j)) for _ in outs]
    out_shape = [jax.ShapeDtypeStruct((m, n), dt) for dt in outs]
    scratch = [pltpu.VMEM((tm, tn), F32) for _ in ws] if nk > 1 else []
    body = functools.partial(_mm_body, nw=len(ws), ne=len(extras), no=len(outs), nk=nk,
                             epilogue=epilogue)
    return pl.pallas_call(
        body,
        grid=(m // tm, n // tn, nk),
        in_specs=in_specs,
        out_specs=out_specs,
        out_shape=out_shape,
        scratch_shapes=scratch,
        compiler_params=_params(("parallel", "parallel", "arbitrary")),
        name="mm" + getattr(epilogue, "func", epilogue).__name__,
    )(x, *ws, *[arr for _, arr in extras])


def _row(vec):
    return ("row", vec.reshape(1, -1))


def _ep_glu(parts, ex, outs):
    a = parts[0] + ex[0][...]
    b = parts[1] + ex[1][...]
    outs[0][...] = a * jax.nn.sigmoid(b)


def _ep_sigmoid_bias(parts, ex, outs):
    outs[0][...] = jax.nn.sigmoid(parts[0] + ex[0][...])


def _rope_slot(a, cos, sin_lo, sin_hi):
    half = ROPE_DIM // 2
    return (a * cos + pltpu.roll(a, ROPE_SLOT - half, axis=1) * sin_lo
            + pltpu.roll(a, half, axis=1) * sin_hi)


def _ep_q(parts, ex, outs, *, scale):
    q = parts[0]
    cos, sin_lo, sin_hi = ex[0][...], ex[1][...], ex[2][...]
    for s in range(q.shape[1] // HEAD_SLOT):
        c0 = s * HEAD_SLOT
        outs[0][:, c0:c0 + NOPE_DIM] = (q[:, c0:c0 + NOPE_DIM] * scale).astype(BF16)
        r = _rope_slot(q[:, c0 + NOPE_DIM:c0 + HEAD_SLOT], cos, sin_lo, sin_hi)
        outs[0][:, c0 + NOPE_DIM:c0 + HEAD_SLOT] = (r * scale).astype(BF16)


def _ep_cast(parts, ex, outs):
    outs[0][...] = parts[0].astype(outs[0].dtype)


def _ep_gate(parts, ex, outs):
    outs[0][...] = ex[0][...] * parts[0]


def _ep_merge(parts, ex, outs):
    outs[0][...] = (ex[0][...] * parts[0] + ex[1][...]).astype(BF16)


def _ep_residual(parts, ex, outs, *, alpha):
    outs[0][...] = alpha * ex[0][...] + parts[0]


def _ep_relu2(parts, ex, outs):
    r = jnp.maximum(parts[0], 0.0)
    outs[0][...] = (r * r).astype(BF16)


def _rms(x, g):
    return x * lax.rsqrt(jnp.mean(x * x, axis=-1, keepdims=True) + RMS_EPS) * g


def _latent_body(x_ref, w_ref, b_ref, qg_ref, kvg_ref, cos_ref, slo_ref, shi_ref,
                 qn_ref, ckv_ref, ckvb_ref, kr_ref, krb_ref, *, q_rank, kv_rank):
    acc = jnp.dot(x_ref[...], w_ref[...], preferred_element_type=F32) + b_ref[...]
    qn_ref[...] = _rms(acc[:, :q_rank], qg_ref[...]).astype(BF16)
    ckv = _rms(acc[:, q_rank:q_rank + kv_rank], kvg_ref[...])
    ckv_ref[...] = ckv
    ckvb_ref[...] = ckv.astype(BF16)
    kr = _rope_slot(acc[:, q_rank + kv_rank:], cos_ref[...], slo_ref[...], shi_ref[...])
    kr_ref[...] = kr[:, :ROPE_DIM]
    krb_ref[...] = kr.astype(BF16)


def _latents(xb, w_lat, b_lat, q_g, kv_g, tabs, *, tm=512):
    m, d = xb.shape
    q_rank, kv_rank = q_g.shape[0], kv_g.shape[0]
    n = w_lat.shape[1]
    tm = _tile(m, tm)
    row = lambda i: (i, 0)
    fix = lambda i: (0, 0)
    body = functools.partial(_latent_body, q_rank=q_rank, kv_rank=kv_rank)
    return pl.pallas_call(
        body,
        grid=(m // tm,),
        in_specs=[pl.BlockSpec((tm, d), row), pl.BlockSpec((d, n), fix),
                  pl.BlockSpec((1, n), fix), pl.BlockSpec((1, q_rank), fix),
                  pl.BlockSpec((1, kv_rank), fix)]
                 + [pl.BlockSpec((tm, ROPE_SLOT), row)] * 3,
        out_specs=[pl.BlockSpec((tm, q_rank), row), pl.BlockSpec((tm, kv_rank), row),
                   pl.BlockSpec((tm, kv_rank), row), pl.BlockSpec((tm, ROPE_DIM), row),
                   pl.BlockSpec((tm, ROPE_SLOT), row)],
        out_shape=[jax.ShapeDtypeStruct((m, q_rank), BF16),
                   jax.ShapeDtypeStruct((m, kv_rank), F32),
                   jax.ShapeDtypeStruct((m, kv_rank), BF16),
                   jax.ShapeDtypeStruct((m, ROPE_DIM), F32),
                   jax.ShapeDtypeStruct((m, ROPE_SLOT), BF16)],
        compiler_params=_params(("parallel",)),
        name="latents",
    )(xb, w_lat, b_lat.reshape(1, n), q_g.reshape(1, -1), kv_g.reshape(1, -1), *tabs)


def _conv_body(hist_ref, u_ref, w_ref, bdw_ref, g_ref, b_ref, o_ref, win_ref, y_ref,
               *, zero_period, lane_chunk):
    c = u_ref.shape[1]
    hist = hist_ref[...]
    if zero_period:
        first = (pl.program_id(0) % zero_period) == 0
        hist = jnp.where(first, 0.0, hist)
    win_ref[0:HIST_ROWS, :] = hist
    win_ref[HIST_ROWS:, :] = u_ref[...]
    for c0 in range(0, c, lane_chunk):
        acc = jnp.zeros((CONV_ROWS, lane_chunk), F32)
        for k in range(CONV_WIDTH):
            acc = acc + (win_ref[HIST_PAD + k:HIST_PAD + k + CONV_ROWS, c0:c0 + lane_chunk]
                         * w_ref[k:k + 1, c0:c0 + lane_chunk])
        y_ref[:, c0:c0 + lane_chunk] = acc + bdw_ref[:, c0:c0 + lane_chunk]
    y = y_ref[...]
    mu = jnp.mean(y, axis=-1, keepdims=True)
    yc = y - mu
    var = jnp.mean(yc * yc, axis=-1, keepdims=True)
    z = yc * lax.rsqrt(var + LN_EPS) * g_ref[...] + b_ref[...]
    o_ref[...] = (z * jax.nn.sigmoid(z)).astype(BF16)


def _conv(u, hist, w_dw, b_dw, g, b, *, zero_period):
    m, c = u.shape
    w_pad = jnp.pad(w_dw, ((0, HIST_ROWS - CONV_WIDTH), (0, 0)))
    per = CONV_ROWS // HIST_ROWS
    if zero_period:
        hist_map = lambda i: (jnp.maximum(i * per - 1, 0), 0)
    else:
        hist_map = lambda i: (i, 0)
    fix = lambda i: (0, 0)
    body = functools.partial(_conv_body, zero_period=zero_period, lane_chunk=min(c, 512))
    return pl.pallas_call(
        body,
        grid=(m // CONV_ROWS,),
        in_specs=[pl.BlockSpec((HIST_ROWS, c), hist_map),
                  pl.BlockSpec((CONV_ROWS, c), lambda i: (i, 0)),
                  pl.BlockSpec((HIST_ROWS, c), fix),
                  pl.BlockSpec((1, c), fix), pl.BlockSpec((1, c), fix), pl.BlockSpec((1, c), fix)],
        out_specs=pl.BlockSpec((CONV_ROWS, c), lambda i: (i, 0)),
        out_shape=jax.ShapeDtypeStruct((m, c), BF16),
        scratch_shapes=[pltpu.VMEM((HIST_ROWS + CONV_ROWS, c), F32),
                        pltpu.VMEM((CONV_ROWS, c), F32)],
        compiler_params=_params(("parallel",)),
        name="conv",
    )(hist, u, w_pad, b_dw.reshape(1, c), g.reshape(1, c), b.reshape(1, c))


def _ln_body(x_ref, g_ref, b_ref, *o_refs):
    x = x_ref[...]
    mu = jnp.mean(x, axis=-1, keepdims=True)
    xc = x - mu
    var = jnp.mean(xc * xc, axis=-1, keepdims=True)
    y = xc * lax.rsqrt(var + LN_EPS) * g_ref[...] + b_ref[...]
    for o in o_refs:
        o[...] = y.astype(o.dtype)


def _layer_norm(x, g, b, dtypes, *, tm=256):
    m, d = x.shape
    tm = _tile(m, tm)
    row = lambda i: (i, 0)
    fix = lambda i: (0, 0)
    return pl.pallas_call(
        _ln_body,
        grid=(m // tm,),
        in_specs=[pl.BlockSpec((tm, d), row), pl.BlockSpec((1, d), fix), pl.BlockSpec((1, d), fix)],
        out_specs=[pl.BlockSpec((tm, d), row) for _ in dtypes],
        out_shape=[jax.ShapeDtypeStruct((m, d), dt) for dt in dtypes],
        compiler_params=_params(("parallel",)),
        name="layer_norm",
    )(x, g.reshape(1, d), b.reshape(1, d))


def _attn_prompt_body(q_ref, kn_ref, kr_ref, v_ref, o_ref, kf_ref, *, tq):
    seq = q_ref.shape[0]
    kf_ref[:, :NOPE_DIM] = kn_ref[...]
    kf_ref[:, NOPE_DIM:] = kr_ref[...]
    row_chunk = lax.broadcasted_iota(jnp.int32, (tq, tq), 0) // CHUNK
    col_chunk = lax.broadcasted_iota(jnp.int32, (tq, tq), 1) // CHUNK
    visible = col_chunk <= row_chunk

    def tile(q, k0, m, l, acc, diagonal):
        k = kf_ref[pl.ds(k0, tq), :]
        v = v_ref[pl.ds(k0, tq), :]
        s = lax.dot_general(q, k, (((1,), (1,)), ((), ())), preferred_element_type=F32)
        if diagonal:
            s = jnp.where(visible, s, NEG_INF)
        m_new = jnp.maximum(m, jnp.max(s, axis=1, keepdims=True))
        alpha = jnp.exp(m - m_new)
        p = jnp.exp(s - m_new)
        l_new = alpha * l + jnp.sum(p, axis=1, keepdims=True)
        acc_new = alpha * acc + jnp.dot(p.astype(BF16), v, preferred_element_type=F32)
        return m_new, l_new, acc_new

    def q_block(qi, carry):
        q0 = pl.multiple_of(qi * tq, tq)
        q = q_ref[pl.ds(q0, tq), :]
        init = (jnp.full((tq, 1), NEG_INF, F32), jnp.zeros((tq, 1), F32),
                jnp.zeros((tq, V_DIM), F32))

        def kv_step(kj, c):
            return tile(q, pl.multiple_of(kj * tq, tq), *c, diagonal=False)

        m, l, acc = lax.fori_loop(0, qi, kv_step, init)
        m, l, acc = tile(q, q0, m, l, acc, diagonal=True)
        o_ref[pl.ds(q0, tq), :] = (acc / l).astype(o_ref.dtype)
        return carry

    lax.fori_loop(0, seq // tq, q_block, 0)


def _attn_prompt(q, kv, kr, *, batch, seq, tq=512):
    tq = _tile(seq, tq)
    assert tq % CHUNK == 0
    per_head = (NOPE_DIM + V_DIM) // LANES
    body = functools.partial(_attn_prompt_body, tq=tq)
    return pl.pallas_call(
        body,
        grid=(batch, N_HEADS),
        in_specs=[pl.BlockSpec((seq, HEAD_SLOT), lambda b, h: (b, h)),
                  pl.BlockSpec((seq, NOPE_DIM), lambda b, h: (b, per_head * h)),
                  pl.BlockSpec((seq, ROPE_SLOT), lambda b, h: (b, 0)),
                  pl.BlockSpec((seq, V_DIM), lambda b, h: (b, per_head * h + 1))],
        out_specs=pl.BlockSpec((seq, V_DIM), lambda b, h: (b, h)),
        out_shape=jax.ShapeDtypeStruct((batch * seq, N_HEADS * V_DIM), BF16),
        scratch_shapes=[pltpu.VMEM((seq, HEAD_SLOT), BF16)],
        compiler_params=_params(("parallel", "parallel")),
        name="attn_prompt",
    )(q, kv, kr, kv)


def _absorb_q_body(qn_ref, qr_ref, wuk_ref, qa_ref, qro_ref, *, batch, t):
    qa = lax.dot_general(qn_ref[...], wuk_ref[...], (((1,), (1,)), ((), ())),
                         preferred_element_type=F32).astype(BF16)
    for b in range(batch):
        qa_ref[b] = qa[b * t:(b + 1) * t]
        qro_ref[b] = qr_ref[b * t:(b + 1) * t, :]


def _absorb_q(q, w_kv, *, batch, t):
    m = batch * t
    kv_rank = w_kv.shape[0]
    per_head = (NOPE_DIM + V_DIM) // LANES
    q_per_head = HEAD_SLOT // LANES
    body = functools.partial(_absorb_q_body, batch=batch, t=t)
    return pl.pallas_call(
        body,
        grid=(N_HEADS,),
        in_specs=[pl.BlockSpec((m, NOPE_DIM), lambda h: (0, q_per_head * h)),
                  pl.BlockSpec((m, ROPE_SLOT), lambda h: (0, q_per_head * h + 1)),
                  pl.BlockSpec((kv_rank, NOPE_DIM), lambda h: (0, per_head * h))],
        out_specs=[pl.BlockSpec((batch, t, kv_rank), lambda h: (0, h, 0)),
                   pl.BlockSpec((batch, t, ROPE_SLOT), lambda h: (0, h, 0))],
        out_shape=[jax.ShapeDtypeStruct((batch, N_HEADS * t, kv_rank), BF16),
                   jax.ShapeDtypeStruct((batch, N_HEADS * t, ROPE_SLOT), BF16)],
        compiler_params=_params(("parallel",)),
        name="absorb_q",
    )(q, q, w_kv)


def _attn_sample_body(qa_ref, qr_ref, cc_ref, ck_ref, nc_ref, nk_ref, o_ref):
    qa, qr = qa_ref[0], qr_ref[0]
    cc, ck, nc, nk = cc_ref[0], ck_ref[0], nc_ref[0], nk_ref[0]
    dims = (((1,), (1,)), ((), ()))
    s_old = (lax.dot_general(qa, cc, dims, preferred_element_type=F32)
             + lax.dot_general(qr, ck, dims, preferred_element_type=F32))
    s_new = (lax.dot_general(qa, nc, dims, preferred_element_type=F32)
             + lax.dot_general(qr, nk, dims, preferred_element_type=F32))
    m = jnp.maximum(jnp.max(s_old, axis=1, keepdims=True), jnp.max(s_new, axis=1, keepdims=True))
    p_old = jnp.exp(s_old - m)
    p_new = jnp.exp(s_new - m)
    l = jnp.sum(p_old, axis=1, keepdims=True) + jnp.sum(p_new, axis=1, keepdims=True)
    o = (jnp.dot(p_old.astype(BF16), cc, preferred_element_type=F32)
         + jnp.dot(p_new.astype(BF16), nc, preferred_element_type=F32))
    o_ref[0] = (o / l).astype(BF16)


def _attn_sample(qa, qr, cache_c, cache_k, new_c, new_k, *, tr=512):
    batch, rows, kv_rank = qa.shape
    past, t = cache_c.shape[1], new_c.shape[1]
    tr = _tile(rows, tr)
    return pl.pallas_call(
        _attn_sample_body,
        grid=(batch, rows // tr),
        in_specs=[pl.BlockSpec((1, tr, kv_rank), lambda b, r: (b, r, 0)),
                  pl.BlockSpec((1, tr, ROPE_SLOT), lambda b, r: (b, r, 0)),
                  pl.BlockSpec((1, past, kv_rank), lambda b, r: (b, 0, 0)),
                  pl.BlockSpec((1, past, ROPE_SLOT), lambda b, r: (b, 0, 0)),
                  pl.BlockSpec((1, t, kv_rank), lambda b, r: (b, 0, 0)),
                  pl.BlockSpec((1, t, ROPE_SLOT), lambda b, r: (b, 0, 0))],
        out_specs=pl.BlockSpec((1, tr, kv_rank), lambda b, r: (b, r, 0)),
        out_shape=jax.ShapeDtypeStruct((batch, rows, kv_rank), BF16),
        compiler_params=_params(("parallel", "parallel")),
        name="attn_sample",
    )(qa, qr, cache_c, cache_k, new_c, new_k)


def _unabsorb_body(o_ref, wuv_ref, out_ref):
    b, t, r = o_ref.shape
    out_ref[...] = jnp.dot(o_ref[...].reshape(b * t, r), wuv_ref[...],
                           preferred_element_type=F32).astype(BF16)


def _unabsorb(o_lat, w_kv, *, t):
    batch, _, kv_rank = o_lat.shape
    per_head = (NOPE_DIM + V_DIM) // LANES
    return pl.pallas_call(
        _unabsorb_body,
        grid=(N_HEADS,),
        in_specs=[pl.BlockSpec((batch, t, kv_rank), lambda h: (0, h, 0)),
                  pl.BlockSpec((kv_rank, V_DIM), lambda h: (0, per_head * h + 1))],
        out_specs=pl.BlockSpec((batch * t, V_DIM), lambda h: (0, h)),
        out_shape=jax.ShapeDtypeStruct((batch * t, N_HEADS * V_DIM), BF16),
        compiler_params=_params(("parallel",)),
        name="unabsorb",
    )(o_lat, w_kv)


def _rope_tables(pos):
    half = ROPE_DIM // 2
    inv = ROPE_THETA ** (-jnp.arange(half, dtype=F32) / half)
    ang = pos.astype(F32)[:, None] * inv[None, :]
    cos, sin = jnp.cos(ang), jnp.sin(ang)
    zero = jnp.zeros_like(cos)
    cos_t = jnp.concatenate([cos, cos, zero, zero], axis=1)
    sin_lo = jnp.concatenate([-sin, zero, zero, zero], axis=1)
    sin_hi = jnp.concatenate([zero, sin, zero, zero], axis=1)
    return cos_t, sin_lo, sin_hi


def _prep_weights(l, w_in, b_in, w_q_b, w_kv_b, d, conv_dim, q_rank, kv_rank):
    w, b = w_in[l], b_in[l]
    o_q = 2 * conv_dim
    o_kv = o_q + q_rank
    o_kr = o_kv + kv_rank
    o_g = o_kr + ROPE_DIM
    pad = ROPE_SLOT - ROPE_DIM
    w_lat = jnp.concatenate([w[:, o_q:o_g], jnp.zeros((d, pad), w.dtype)], axis=1).astype(BF16)
    b_lat = jnp.concatenate([b[o_q:o_g], jnp.zeros((pad,), b.dtype)])
    wq = w_q_b[l].reshape(q_rank, N_HEADS, QK_DIM)
    wq = jnp.pad(wq, ((0, 0), (0, 0), (0, HEAD_SLOT - QK_DIM))).reshape(q_rank, N_HEADS * HEAD_SLOT)
    return dict(
        w_ga=w[:, :conv_dim].astype(BF16), b_ga=b[:conv_dim],
        w_gb=w[:, conv_dim:o_q].astype(BF16), b_gb=b[conv_dim:o_q],
        w_lat=w_lat, b_lat=b_lat,
        w_gate=w[:, o_g:].astype(BF16), b_gate=b[o_g:],
        w_q=wq.astype(BF16), w_kv=w_kv_b[l].astype(BF16))


def _token_mixer(x, wts, p, *, alpha):
    residual = functools.partial(_ep_residual, alpha=alpha)
    gconv = _matmul(p["conv_act"], [wts["w_pw"]], [(0, p["gates"])], [F32], _ep_gate)[0]
    merged = _matmul(p["attn"], [wts["w_o"]], [(1, p["gates"]), (0, gconv)], [BF16], _ep_merge,
                     tm=512, tn=512, tk=8192)[0]
    pre1 = _matmul(merged, [wts["w_out"]], [(0, x)], [F32], residual)[0]
    h, hb = _layer_norm(pre1, wts["ln1_g"], wts["ln1_b"], [F32, BF16])
    act = _matmul(hb, [wts["w_up"]], [], [BF16], _ep_relu2)[0]
    pre2 = _matmul(act, [wts["w_down"]], [(0, h)], [F32], residual, tk=2048)[0]
    return _layer_norm(pre2, wts["ln2_g"], wts["ln2_b"], [F32])[0]


def _in_stage(x, tabs, wts, hist, *, zero_period, scale):
    xb = x.astype(BF16)
    u = _matmul(xb, [wts["w_ga"], wts["w_gb"]], [_row(wts["b_ga"]), _row(wts["b_gb"])],
                [F32], _ep_glu, tn=512)[0]
    gates = _matmul(xb, [wts["w_gate"]], [_row(wts["b_gate"])], [F32], _ep_sigmoid_bias)[0]
    qn, ckv, ckvb, kr, krb = _latents(xb, wts["w_lat"], wts["b_lat"], wts["q_a_g"],
                                      wts["kv_a_g"], tabs)
    conv_act = _conv(u, u if zero_period else hist, wts["w_dw"], wts["b_dw"],
                     wts["conv_ln_g"], wts["conv_ln_b"], zero_period=zero_period)
    q = _matmul(qn, [wts["w_q"]], [("rows", t) for t in tabs], [BF16],
                functools.partial(_ep_q, scale=scale))[0]
    return dict(u=u, gates=gates, ckv=ckv, ckvb=ckvb, kr=kr, krb=krb, conv_act=conv_act, q=q)


def kernel(x_prompt, x_sample, cache_ckv, cache_krope, state_conv, w_in, b_in, w_dw, b_dw,
           conv_ln_g, conv_ln_b, w_conv_pw, q_a_g, w_q_b, kv_a_g, w_kv_b, w_attn_o, w_out,
           ln1_g, ln1_b, w_up, w_down, ln2_g, ln2_b):
    depth = w_in.shape[0]
    bp, sp, d = x_prompt.shape
    bs, ts, _ = x_sample.shape
    past = cache_ckv.shape[2]
    conv_dim = w_dw.shape[2]
    q_rank, kv_rank = q_a_g.shape[1], kv_a_g.shape[1]
    alpha = (2 * depth) ** 0.25
    scale = QK_DIM ** -0.5
    hist_len = CONV_WIDTH - 1
    assert ts == CONV_ROWS and sp % CONV_ROWS == 0

    tabs_p = [jnp.tile(t, (bp, 1)) for t in _rope_tables(jnp.arange(sp))]
    tabs_s = [jnp.tile(t, (bs, 1)) for t in _rope_tables(past + jnp.arange(ts))]

    hp = x_prompt.reshape(bp * sp, d)
    hs = x_sample.reshape(bs * ts, d)
    outs = [[] for _ in range(6)]
    for l in range(depth):
        wts = _prep_weights(l, w_in, b_in, w_q_b, w_kv_b, d, conv_dim, q_rank, kv_rank)
        wts.update(
            w_dw=w_dw[l], b_dw=b_dw[l], conv_ln_g=conv_ln_g[l], conv_ln_b=conv_ln_b[l],
            q_a_g=q_a_g[l], kv_a_g=kv_a_g[l], w_pw=w_conv_pw[l].astype(BF16),
            w_o=w_attn_o[l].astype(BF16), w_out=w_out[l].astype(BF16),
            ln1_g=ln1_g[l], ln1_b=ln1_b[l], w_up=w_up[l].astype(BF16),
            w_down=w_down[l].astype(BF16), ln2_g=ln2_g[l], ln2_b=ln2_b[l])

        p = _in_stage(hp, tabs_p, wts, None, zero_period=sp // CONV_ROWS, scale=scale)
        kv = _matmul(p["ckvb"], [wts["w_kv"]], [], [BF16], _ep_cast)[0]
        p["attn"] = _attn_prompt(p["q"], kv, p["krb"], batch=bp, seq=sp)
        hp_new = _token_mixer(hp, wts, p, alpha=alpha)
        outs[0].append(p["ckv"].reshape(bp, sp, kv_rank))
        outs[1].append(p["kr"].reshape(bp, sp, ROPE_DIM))
        outs[2].append(p["u"].reshape(bp, sp, conv_dim)[:, sp - hist_len:])

        hist = jnp.pad(state_conv[l], ((0, 0), (HIST_PAD, 0), (0, 0))).reshape(bs * HIST_ROWS, conv_dim)
        s = _in_stage(hs, tabs_s, wts, hist, zero_period=0, scale=scale)
        qa, qr = _absorb_q(s["q"], wts["w_kv"], batch=bs, t=ts)
        cache_k = jnp.pad(cache_krope[l], ((0, 0), (0, 0), (0, ROPE_SLOT - ROPE_DIM))).astype(BF16)
        o_lat = _attn_sample(qa, qr, cache_ckv[l].astype(BF16), cache_k,
                             s["ckvb"].reshape(bs, ts, kv_rank), s["krb"].reshape(bs, ts, ROPE_SLOT))
        s["attn"] = _unabsorb(o_lat, wts["w_kv"], t=ts)
        hs_new = _token_mixer(hs, wts, s, alpha=alpha)
        outs[3].append(s["ckv"].reshape(bs, ts, kv_rank))
        outs[4].append(s["kr"].reshape(bs, ts, ROPE_DIM))
        outs[5].append(s["u"].reshape(bs, ts, conv_dim)[:, ts - hist_len:])
        hp, hs = hp_new, hs_new

    return (hp.reshape(bp, sp, d), hs.reshape(bs, ts, d), jnp.stack(outs[0]), jnp.stack(outs[1]),
            jnp.stack(outs[2]), jnp.stack(outs[3]), jnp.stack(outs[4]), jnp.stack(outs[5]))
```

```python
import functools
import math

import jax
import jax.numpy as jnp
from jax import lax
from jax.experimental import pallas as pl
from jax.experimental.pallas import tpu as pltpu

F32 = jnp.float32
BF16 = jnp.bfloat16

CHUNK = 64
CONV_WIDTH = 31
N_HEADS = 64
NOPE_DIM = 128
ROPE_DIM = 64
V_DIM = 128
QK_DIM = NOPE_DIM + ROPE_DIM
ROPE_THETA = 10000.0
LN_EPS = 1e-5
RMS_EPS = 1e-6
NEG_INF = -1e30

LANES = 128
SUBLANES = 8
VMEM_LIMIT_BYTES = 60 * 1024 * 1024

ROPE_SLOT = LANES
HEAD_SLOT = NOPE_DIM + ROPE_SLOT
HIST_ROWS = 32
HIST_PAD = HIST_ROWS - (CONV_WIDTH - 1)
CONV_ROWS = 64


def _tile(dim, pref):
    t = min(dim, pref)
    assert dim % t == 0, (dim, pref)
    return t


def _params(sem):
    return pltpu.CompilerParams(dimension_semantics=sem, vmem_limit_bytes=VMEM_LIMIT_BYTES)


def _mm_body(*refs, nw, ne, no, nk, epilogue):
    x_ref = refs[0]
    w_refs = refs[1:1 + nw]
    ex = refs[1 + nw:1 + nw + ne]
    outs = refs[1 + nw + ne:1 + nw + ne + no]
    accs = refs[1 + nw + ne + no:]
    parts = [jnp.dot(x_ref[...], w[...], preferred_element_type=F32) for w in w_refs]
    if nk == 1:
        epilogue(parts, ex, outs)
        return
    k = pl.program_id(2)

    @pl.when(k == 0)
    def _():
        for a, p in zip(accs, parts):
            a[...] = p

    @pl.when(jnp.logical_and(k > 0, k < nk - 1))
    def _():
        for a, p in zip(accs, parts):
            a[...] += p

    @pl.when(k == nk - 1)
    def _():
        epilogue([a[...] + p for a, p in zip(accs, parts)], ex, outs)


def _extra_spec(kind, arr, tm, tn, n):
    if kind == "row":
        return pl.BlockSpec((1, tn), lambda i, j, k: (0, j))
    if kind == "rows":
        return pl.BlockSpec((tm, arr.shape[1]), lambda i, j, k: (i, 0))
    off = kind * (n // tn)
    return pl.BlockSpec((tm, tn), lambda i, j, k: (i, j + off))


def _matmul(x, ws, extras, outs, epilogue, *, tm=1024, tn=1024, tk=4096):
    m, kdim = x.shape
    n = ws[0].shape[1]
    tm, tn, tk = _tile(m, tm), _tile(n, tn), _tile(kdim, tk)
    nk = kdim // tk
    in_specs = [pl.BlockSpec((tm, tk), lambda i, j, k: (i, k))]
    in_specs += [pl.BlockSpec((tk, tn), lambda i, j, k: (k, j)) for _ in ws]
    in_specs += [_extra_spec(kind, arr, tm, tn, n) for kind, arr in extras]
    out_specs = [pl.BlockSpec((tm, tn), lambda i, j, k: (i, j)) for _ in outs]
    out_shape = [jax.ShapeDtypeStruct((m, n), dt) for dt in outs]
    scratch = [pltpu.VMEM((tm, tn), F32) for _ in ws] if nk > 1 else []
    body = functools.partial(_mm_body, nw=len(ws), ne=len(extras), no=len(outs), nk=nk,
                             epilogue=epilogue)
    return pl.pallas_call(
        body,
        grid=(m // tm, n // tn, nk),
        in_specs=in_specs,
        out_specs=out_specs,
        out_shape=out_shape,
        scratch_shapes=scratch,
        compiler_params=_params(("parallel", "parallel", "arbitrary")),
        name="mm" + getattr(epilogue, "func", epilogue).__name__,
    )(x, *ws, *[arr for _, arr in extras])


def _row(vec):
    return ("row", vec.reshape(1, -1))


def _ep_glu(parts, ex, outs):
    a = parts[0] + ex[0][...]
    b = parts[1] + ex[1][...]
    outs[0][...] = a * jax.nn.sigmoid(b)


def _ep_sigmoid_bias(parts, ex, outs):
    outs[0][...] = jax.nn.sigmoid(parts[0] + ex[0][...])


def _rope_slot(a, cos, sin_lo, sin_hi):
    half = ROPE_DIM // 2
    return (a * cos + pltpu.roll(a, ROPE_SLOT - half, axis=1) * sin_lo
            + pltpu.roll(a, half, axis=1) * sin_hi)


def _ep_q(parts, ex, outs, *, scale):
    q = parts[0]
    cos, sin_lo, sin_hi = ex[0][...], ex[1][...], ex[2][...]
    for s in range(q.shape[1] // HEAD_SLOT):
        c0 = s * HEAD_SLOT
        outs[0][:, c0:c0 + NOPE_DIM] = (q[:, c0:c0 + NOPE_DIM] * scale).astype(BF16)
        r = _rope_slot(q[:, c0 + NOPE_DIM:c0 + HEAD_SLOT], cos, sin_lo, sin_hi)
        outs[0][:, c0 + NOPE_DIM:c0 + HEAD_SLOT] = (r * scale).astype(BF16)


def _ep_cast(parts, ex, outs):
    outs[0][...] = parts[0].astype(outs[0].dtype)


def _ep_gate(parts, ex, outs):
    outs[0][...] = ex[0][...] * parts[0]


def _ep_merge(parts, ex, outs):
    outs[0][...] = (ex[0][...] * parts[0] + ex[1][...]).astype(BF16)


def _ep_residual(parts, ex, outs, *, alpha):
    outs[0][...] = alpha * ex[0][...] + parts[0]


def _ep_relu2(parts, ex, outs):
    r = jnp.maximum(parts[0], 0.0)
    outs[0][...] = (r * r).astype(BF16)


def _rms(x, g):
    return x * lax.rsqrt(jnp.mean(x * x, axis=-1, keepdims=True) + RMS_EPS) * g


def _latent_body(x_ref, w_ref, b_ref, qg_ref, kvg_ref, cos_ref, slo_ref, shi_ref,
                 qn_ref, ckv_ref, ckvb_ref, kr_ref, krb_ref, *, q_rank, kv_rank):
    acc = jnp.dot(x_ref[...], w_ref[...], preferred_element_type=F32) + b_ref[...]
    qn_ref[...] = _rms(acc[:, :q_rank], qg_ref[...]).astype(BF16)
    ckv = _rms(acc[:, q_rank:q_rank + kv_rank], kvg_ref[...])
    ckv_ref[...] = ckv
    ckvb_ref[...] = ckv.astype(BF16)
    kr = _rope_slot(acc[:, q_rank + kv_rank:], cos_ref[...], slo_ref[...], shi_ref[...])
    kr_ref[...] = kr[:, :ROPE_DIM]
    krb_ref[...] = kr.astype(BF16)


def _latents(xb, w_lat, b_lat, q_g, kv_g, tabs, *, tm=512):
    m, d = xb.shape
    q_rank, kv_rank = q_g.shape[0], kv_g.shape[0]
    n = w_lat.shape[1]
    tm = _tile(m, tm)
    row = lambda i: (i, 0)
    fix = lambda i: (0, 0)
    body = functools.partial(_latent_body, q_rank=q_rank, kv_rank=kv_rank)
    return pl.pallas_call(
        body,
        grid=(m // tm,),
        in_specs=[pl.BlockSpec((tm, d), row), pl.BlockSpec((d, n), fix),
                  pl.BlockSpec((1, n), fix), pl.BlockSpec((1, q_rank), fix),
                  pl.BlockSpec((1, kv_rank), fix)]
                 + [pl.BlockSpec((tm, ROPE_SLOT), row)] * 3,
        out_specs=[pl.BlockSpec((tm, q_rank), row), pl.BlockSpec((tm, kv_rank), row),
                   pl.BlockSpec((tm, kv_rank), row), pl.BlockSpec((tm, ROPE_DIM), row),
                   pl.BlockSpec((tm, ROPE_SLOT), row)],
        out_shape=[jax.ShapeDtypeStruct((m, q_rank), BF16),
                   jax.ShapeDtypeStruct((m, kv_rank), F32),
                   jax.ShapeDtypeStruct((m, kv_rank), BF16),
                   jax.ShapeDtypeStruct((m, ROPE_DIM), F32),
                   jax.ShapeDtypeStruct((m, ROPE_SLOT), BF16)],
        compiler_params=_params(("parallel",)),
        name="latents",
    )(xb, w_lat, b_lat.reshape(1, n), q_g.reshape(1, -1), kv_g.reshape(1, -1), *tabs)


def _conv_body(hist_ref, u_ref, w_ref, bdw_ref, g_ref, b_ref, o_ref, win_ref, y_ref,
               *, zero_period, lane_chunk):
    c = u_ref.shape[1]
    hist = hist_ref[...]
    if zero_period:
        first = (pl.program_id(0) % zero_period) == 0
        hist = jnp.where(first, 0.0, hist)
    win_ref[0:HIST_ROWS, :] = hist
    win_ref[HIST_ROWS:, :] = u_ref[...]
    for c0 in range(0, c, lane_chunk):
        lanes = slice(c0, c0 + lane_chunk)
        acc = bdw_ref[:, lanes]
        for r in range(SUBLANES):
            rows = CONV_ROWS if r == 0 else CONV_ROWS + SUBLANES
            z = None
            for a in range((HIST_PAD + CONV_WIDTH - 1 - r) // SUBLANES + 1):
                k = a * SUBLANES + r - HIST_PAD
                if k < 0:
                    continue
                term = win_ref[a * SUBLANES:a * SUBLANES + rows, lanes] * w_ref[k:k + 1, lanes]
                z = term if z is None else z + term
            acc = acc + z[r:r + CONV_ROWS]
        y_ref[:, lanes] = acc
    y = y_ref[...]
    mu = jnp.mean(y, axis=-1, keepdims=True)
    yc = y - mu
    var = jnp.mean(yc * yc, axis=-1, keepdims=True)
    z = yc * lax.rsqrt(var + LN_EPS) * g_ref[...] + b_ref[...]
    o_ref[...] = (z * jax.nn.sigmoid(z)).astype(BF16)


def _conv(u, hist, w_dw, b_dw, g, b, *, zero_period):
    m, c = u.shape
    w_pad = jnp.pad(w_dw, ((0, HIST_ROWS - CONV_WIDTH), (0, 0)))
    per = CONV_ROWS // HIST_ROWS
    if zero_period:
        hist_map = lambda i: (jnp.maximum(i * per - 1, 0), 0)
    else:
        hist_map = lambda i: (i, 0)
    fix = lambda i: (0, 0)
    body = functools.partial(_conv_body, zero_period=zero_period, lane_chunk=min(c, 2 * LANES))
    return pl.pallas_call(
        body,
        grid=(m // CONV_ROWS,),
        in_specs=[pl.BlockSpec((HIST_ROWS, c), hist_map),
                  pl.BlockSpec((CONV_ROWS, c), lambda i: (i, 0)),
                  pl.BlockSpec((HIST_ROWS, c), fix),
                  pl.BlockSpec((1, c), fix), pl.BlockSpec((1, c), fix), pl.BlockSpec((1, c), fix)],
        out_specs=pl.BlockSpec((CONV_ROWS, c), lambda i: (i, 0)),
        out_shape=jax.ShapeDtypeStruct((m, c), BF16),
        scratch_shapes=[pltpu.VMEM((HIST_ROWS + CONV_ROWS, c), F32),
                        pltpu.VMEM((CONV_ROWS, c), F32)],
        compiler_params=_params(("parallel",)),
        name="conv",
    )(hist, u, w_pad, b_dw.reshape(1, c), g.reshape(1, c), b.reshape(1, c))


def _ln_body(x_ref, g_ref, b_ref, *o_refs):
    x = x_ref[...]
    mu = jnp.mean(x, axis=-1, keepdims=True)
    xc = x - mu
    var = jnp.mean(xc * xc, axis=-1, keepdims=True)
    y = xc * lax.rsqrt(var + LN_EPS) * g_ref[...] + b_ref[...]
    for o in o_refs:
        o[...] = y.astype(o.dtype)


def _layer_norm(x, g, b, dtypes, *, tm=256):
    m, d = x.shape
    tm = _tile(m, tm)
    row = lambda i: (i, 0)
    fix = lambda i: (0, 0)
    return pl.pallas_call(
        _ln_body,
        grid=(m // tm,),
        in_specs=[pl.BlockSpec((tm, d), row), pl.BlockSpec((1, d), fix), pl.BlockSpec((1, d), fix)],
        out_specs=[pl.BlockSpec((tm, d), row) for _ in dtypes],
        out_shape=[jax.ShapeDtypeStruct((m, d), dt) for dt in dtypes],
        compiler_params=_params(("parallel",)),
        name="layer_norm",
    )(x, g.reshape(1, d), b.reshape(1, d))


def _row_reduce(x, combine, reduce):
    part = x[:, :LANES]
    for c in range(LANES, x.shape[1], LANES):
        part = combine(part, x[:, c:c + LANES])
    return reduce(part, axis=1, keepdims=True)


def _attn_prompt_body(q_ref, kn_ref, kr_ref, v_ref, o_ref, kf_ref, vt_ref, s0_ref, s1_ref, p0_ref,
                      p1_ref, acc_ref, m_ref, l_ref, a_ref, *, tq):
    seq = q_ref.shape[0]
    kf_ref[:, :NOPE_DIM] = kn_ref[...]
    kf_ref[:, NOPE_DIM:] = kr_ref[...]
    for t in range(seq // tq):
        vt_ref[t] = v_ref[t * tq:(t + 1) * tq, :].T
    key_chunk = lax.broadcasted_iota(jnp.int32, (tq, tq), 0) // CHUNK
    q_chunk = lax.broadcasted_iota(jnp.int32, (tq, tq), 1) // CHUNK
    visible = key_chunk <= q_chunk

    def rows(ref, j):
        return ref[pl.ds(pl.multiple_of(j * tq, tq), tq), :]

    def scores(qi, kj):
        return lax.dot_general(rows(kf_ref, kj), rows(q_ref, qi), (((1,), (1,)), ((), ())),
                               preferred_element_type=F32)

    def step(qi, j, s_cur, s_nxt, p_cur, p_prv, last):
        if not last:
            s_nxt[...] = scores(qi, j + 1)
        pv_prev = jnp.dot(vt_ref[jnp.maximum(j - 1, 0)], p_prv[...], preferred_element_type=F32)
        s = s_cur[...]
        if last:
            s = jnp.where(visible, s, NEG_INF)
        m_old = m_ref[...]
        m_new = jnp.maximum(m_old, jnp.max(s, axis=0, keepdims=True))
        alpha = jnp.exp2(m_old - m_new)
        p = jnp.exp2(s - m_new)
        l_ref[...] = alpha * l_ref[...] + jnp.sum(p, axis=0, keepdims=True)
        m_ref[...] = m_new
        p_cur[...] = p.astype(BF16)
        acc = acc_ref[...] * a_ref[...] + pv_prev
        if last:
            acc = acc * alpha + jnp.dot(vt_ref[j], p_cur[...], preferred_element_type=F32)
            out_t = acc / l_ref[...]
            o_ref[pl.ds(pl.multiple_of(qi * tq, tq), tq), :] = out_t.T.astype(o_ref.dtype)
        else:
            acc_ref[...] = acc
            a_ref[...] = alpha

    def q_block(qi, carry):
        s0_ref[...] = scores(qi, 0)
        p1_ref[...] = jnp.zeros_like(p1_ref)
        acc_ref[...] = jnp.zeros_like(acc_ref)
        m_ref[...] = jnp.full_like(m_ref, NEG_INF)
        l_ref[...] = jnp.zeros_like(l_ref)
        a_ref[...] = jnp.ones_like(a_ref)

        def two_steps(jj, c):
            step(qi, 2 * jj, s0_ref, s1_ref, p0_ref, p1_ref, False)
            step(qi, 2 * jj + 1, s1_ref, s0_ref, p1_ref, p0_ref, False)
            return c

        lax.fori_loop(0, qi // 2, two_steps, 0)

        @pl.when(qi % 2 == 1)
        def _():
            step(qi, qi - 1, s0_ref, s1_ref, p0_ref, p1_ref, False)
            step(qi, qi, s1_ref, s0_ref, p1_ref, p0_ref, True)

        @pl.when(qi % 2 == 0)
        def _():
            step(qi, qi, s0_ref, s1_ref, p0_ref, p1_ref, True)

        return carry

    lax.fori_loop(0, seq // tq, q_block, 0)


def _attn_prompt(q, kv, kr, *, batch, seq, tq=512):
    tq = _tile(seq, tq)
    assert tq % CHUNK == 0
    per_head = (NOPE_DIM + V_DIM) // LANES
    body = functools.partial(_attn_prompt_body, tq=tq)
    return pl.pallas_call(
        body,
        grid=(batch, N_HEADS),
        in_specs=[pl.BlockSpec((seq, HEAD_SLOT), lambda b, h: (b, h)),
                  pl.BlockSpec((seq, NOPE_DIM), lambda b, h: (b, per_head * h)),
                  pl.BlockSpec((seq, ROPE_SLOT), lambda b, h: (b, 0)),
                  pl.BlockSpec((seq, V_DIM), lambda b, h: (b, per_head * h + 1))],
        out_specs=pl.BlockSpec((seq, V_DIM), lambda b, h: (b, h)),
        out_shape=jax.ShapeDtypeStruct((batch * seq, N_HEADS * V_DIM), BF16),
        scratch_shapes=[pltpu.VMEM((seq, HEAD_SLOT), BF16),
                        pltpu.VMEM((seq // tq, V_DIM, tq), BF16),
                        pltpu.VMEM((tq, tq), F32), pltpu.VMEM((tq, tq), F32),
                        pltpu.VMEM((tq, tq), BF16), pltpu.VMEM((tq, tq), BF16),
                        pltpu.VMEM((V_DIM, tq), F32), pltpu.VMEM((1, tq), F32),
                        pltpu.VMEM((1, tq), F32), pltpu.VMEM((1, tq), F32)],
        compiler_params=_params(("parallel", "parallel")),
        name="attn_prompt",
    )(q, kv, kr, kv)


def _absorb_q_body(qn_ref, qr_ref, wuk_ref, qa_ref, qro_ref, *, batch, t):
    qa = lax.dot_general(qn_ref[...], wuk_ref[...], (((1,), (1,)), ((), ())),
                         preferred_element_type=F32).astype(BF16)
    for b in range(batch):
        qa_ref[b] = qa[b * t:(b + 1) * t]
        qro_ref[b] = qr_ref[b * t:(b + 1) * t, :]


def _absorb_q(q, w_kv, *, batch, t):
    m = batch * t
    kv_rank = w_kv.shape[0]
    per_head = (NOPE_DIM + V_DIM) // LANES
    q_per_head = HEAD_SLOT // LANES
    body = functools.partial(_absorb_q_body, batch=batch, t=t)
    return pl.pallas_call(
        body,
        grid=(N_HEADS,),
        in_specs=[pl.BlockSpec((m, NOPE_DIM), lambda h: (0, q_per_head * h)),
                  pl.BlockSpec((m, ROPE_SLOT), lambda h: (0, q_per_head * h + 1)),
                  pl.BlockSpec((kv_rank, NOPE_DIM), lambda h: (0, per_head * h))],
        out_specs=[pl.BlockSpec((batch, t, kv_rank), lambda h: (0, h, 0)),
                   pl.BlockSpec((batch, t, ROPE_SLOT), lambda h: (0, h, 0))],
        out_shape=[jax.ShapeDtypeStruct((batch, N_HEADS * t, kv_rank), BF16),
                   jax.ShapeDtypeStruct((batch, N_HEADS * t, ROPE_SLOT), BF16)],
        compiler_params=_params(("parallel",)),
        name="absorb_q",
    )(q, q, w_kv)


def _attn_sample_body(qa_ref, qr_ref, cc_ref, ck_ref, nc_ref, nk_ref, o_ref):
    qa, qr = qa_ref[0], qr_ref[0]
    cc, ck, nc, nk = cc_ref[0], ck_ref[0], nc_ref[0], nk_ref[0]
    dims = (((1,), (1,)), ((), ()))
    s_old = (lax.dot_general(qa, cc, dims, preferred_element_type=F32)
             + lax.dot_general(qr, ck, dims, preferred_element_type=F32))
    s_new = (lax.dot_general(qa, nc, dims, preferred_element_type=F32)
             + lax.dot_general(qr, nk, dims, preferred_element_type=F32))
    m = jnp.maximum(_row_reduce(s_old, jnp.maximum, jnp.max),
                    jnp.max(s_new, axis=1, keepdims=True))
    p_old = jnp.exp2(s_old - m)
    p_new = jnp.exp2(s_new - m)
    l = _row_reduce(p_old, jnp.add, jnp.sum) + jnp.sum(p_new, axis=1, keepdims=True)
    o = (jnp.dot(p_old.astype(BF16), cc, preferred_element_type=F32)
         + jnp.dot(p_new.astype(BF16), nc, preferred_element_type=F32))
    o_ref[0] = (o / l).astype(BF16)


def _attn_sample(qa, qr, cache_c, cache_k, new_c, new_k, *, tr=512):
    batch, rows, kv_rank = qa.shape
    past, t = cache_c.shape[1], new_c.shape[1]
    tr = _tile(rows, tr)
    return pl.pallas_call(
        _attn_sample_body,
        grid=(batch, rows // tr),
        in_specs=[pl.BlockSpec((1, tr, kv_rank), lambda b, r: (b, r, 0)),
                  pl.BlockSpec((1, tr, ROPE_SLOT), lambda b, r: (b, r, 0)),
                  pl.BlockSpec((1, past, kv_rank), lambda b, r: (b, 0, 0)),
                  pl.BlockSpec((1, past, ROPE_SLOT), lambda b, r: (b, 0, 0)),
                  pl.BlockSpec((1, t, kv_rank), lambda b, r: (b, 0, 0)),
                  pl.BlockSpec((1, t, ROPE_SLOT), lambda b, r: (b, 0, 0))],
        out_specs=pl.BlockSpec((1, tr, kv_rank), lambda b, r: (b, r, 0)),
        out_shape=jax.ShapeDtypeStruct((batch, rows, kv_rank), BF16),
        compiler_params=_params(("parallel", "parallel")),
        name="attn_sample",
    )(qa, qr, cache_c, cache_k, new_c, new_k)


def _unabsorb_body(o_ref, wuv_ref, out_ref):
    b, t, r = o_ref.shape
    out_ref[...] = jnp.dot(o_ref[...].reshape(b * t, r), wuv_ref[...],
                           preferred_element_type=F32).astype(BF16)


def _unabsorb(o_lat, w_kv, *, t):
    batch, _, kv_rank = o_lat.shape
    per_head = (NOPE_DIM + V_DIM) // LANES
    return pl.pallas_call(
        _unabsorb_body,
        grid=(N_HEADS,),
        in_specs=[pl.BlockSpec((batch, t, kv_rank), lambda h: (0, h, 0)),
                  pl.BlockSpec((kv_rank, V_DIM), lambda h: (0, per_head * h + 1))],
        out_specs=pl.BlockSpec((batch * t, V_DIM), lambda h: (0, h)),
        out_shape=jax.ShapeDtypeStruct((batch * t, N_HEADS * V_DIM), BF16),
        compiler_params=_params(("parallel",)),
        name="unabsorb",
    )(o_lat, w_kv)


def _rope_tables(pos):
    half = ROPE_DIM // 2
    inv = ROPE_THETA ** (-jnp.arange(half, dtype=F32) / half)
    ang = pos.astype(F32)[:, None] * inv[None, :]
    cos, sin = jnp.cos(ang), jnp.sin(ang)
    zero = jnp.zeros_like(cos)
    cos_t = jnp.concatenate([cos, cos, zero, zero], axis=1)
    sin_lo = jnp.concatenate([-sin, zero, zero, zero], axis=1)
    sin_hi = jnp.concatenate([zero, sin, zero, zero], axis=1)
    return cos_t, sin_lo, sin_hi


def _prep_weights(l, w_in, b_in, w_q_b, w_kv_b, d, conv_dim, q_rank, kv_rank):
    w, b = w_in[l], b_in[l]
    o_q = 2 * conv_dim
    o_kv = o_q + q_rank
    o_kr = o_kv + kv_rank
    o_g = o_kr + ROPE_DIM
    pad = ROPE_SLOT - ROPE_DIM
    w_lat = jnp.concatenate([w[:, o_q:o_g], jnp.zeros((d, pad), w.dtype)], axis=1).astype(BF16)
    b_lat = jnp.concatenate([b[o_q:o_g], jnp.zeros((pad,), b.dtype)])
    wq = w_q_b[l].reshape(q_rank, N_HEADS, QK_DIM)
    wq = jnp.pad(wq, ((0, 0), (0, 0), (0, HEAD_SLOT - QK_DIM))).reshape(q_rank, N_HEADS * HEAD_SLOT)
    return dict(
        w_ga=w[:, :conv_dim].astype(BF16), b_ga=b[:conv_dim],
        w_gb=w[:, conv_dim:o_q].astype(BF16), b_gb=b[conv_dim:o_q],
        w_lat=w_lat, b_lat=b_lat,
        w_gate=w[:, o_g:].astype(BF16), b_gate=b[o_g:],
        w_q=wq.astype(BF16), w_kv=w_kv_b[l].astype(BF16))


def _token_mixer(x, wts, p, *, alpha):
    residual = functools.partial(_ep_residual, alpha=alpha)
    gconv = _matmul(p["conv_act"], [wts["w_pw"]], [(0, p["gates"])], [F32], _ep_gate)[0]
    merged = _matmul(p["attn"], [wts["w_o"]], [(1, p["gates"]), (0, gconv)], [BF16], _ep_merge,
                     tm=512, tn=512, tk=8192)[0]
    pre1 = _matmul(merged, [wts["w_out"]], [(0, x)], [F32], residual)[0]
    h, hb = _layer_norm(pre1, wts["ln1_g"], wts["ln1_b"], [F32, BF16])
    act = _matmul(hb, [wts["w_up"]], [], [BF16], _ep_relu2)[0]
    pre2 = _matmul(act, [wts["w_down"]], [(0, h)], [F32], residual, tk=2048)[0]
    return _layer_norm(pre2, wts["ln2_g"], wts["ln2_b"], [F32])[0]


def _in_stage(x, tabs, wts, hist, *, zero_period, scale):
    xb = x.astype(BF16)
    u = _matmul(xb, [wts["w_ga"], wts["w_gb"]], [_row(wts["b_ga"]), _row(wts["b_gb"])],
                [F32], _ep_glu, tn=512)[0]
    gates = _matmul(xb, [wts["w_gate"]], [_row(wts["b_gate"])], [F32], _ep_sigmoid_bias)[0]
    qn, ckv, ckvb, kr, krb = _latents(xb, wts["w_lat"], wts["b_lat"], wts["q_a_g"],
                                      wts["kv_a_g"], tabs)
    conv_act = _conv(u, u if zero_period else hist, wts["w_dw"], wts["b_dw"],
                     wts["conv_ln_g"], wts["conv_ln_b"], zero_period=zero_period)
    q = _matmul(qn, [wts["w_q"]], [("rows", t) for t in tabs], [BF16],
                functools.partial(_ep_q, scale=scale))[0]
    return dict(u=u, gates=gates, ckv=ckv, ckvb=ckvb, kr=kr, krb=krb, conv_act=conv_act, q=q)


def kernel(x_prompt, x_sample, cache_ckv, cache_krope, state_conv, w_in, b_in, w_dw, b_dw,
           conv_ln_g, conv_ln_b, w_conv_pw, q_a_g, w_q_b, kv_a_g, w_kv_b, w_attn_o, w_out,
           ln1_g, ln1_b, w_up, w_down, ln2_g, ln2_b):
    depth = w_in.shape[0]
    bp, sp, d = x_prompt.shape
    bs, ts, _ = x_sample.shape
    past = cache_ckv.shape[2]
    conv_dim = w_dw.shape[2]
    q_rank, kv_rank = q_a_g.shape[1], kv_a_g.shape[1]
    alpha = (2 * depth) ** 0.25
    scale = QK_DIM ** -0.5 * math.log2(math.e)
    hist_len = CONV_WIDTH - 1
    assert ts == CONV_ROWS and sp % CONV_ROWS == 0

    tabs_p = [jnp.tile(t, (bp, 1)) for t in _rope_tables(jnp.arange(sp))]
    tabs_s = [jnp.tile(t, (bs, 1)) for t in _rope_tables(past + jnp.arange(ts))]

    hp = x_prompt.reshape(bp * sp, d)
    hs = x_sample.reshape(bs * ts, d)
    outs = [[] for _ in range(6)]
    for l in range(depth):
        wts = _prep_weights(l, w_in, b_in, w_q_b, w_kv_b, d, conv_dim, q_rank, kv_rank)
        wts.update(
            w_dw=w_dw[l], b_dw=b_dw[l], conv_ln_g=conv_ln_g[l], conv_ln_b=conv_ln_b[l],
            q_a_g=q_a_g[l], kv_a_g=kv_a_g[l], w_pw=w_conv_pw[l].astype(BF16),
            w_o=w_attn_o[l].astype(BF16), w_out=w_out[l].astype(BF16),
            ln1_g=ln1_g[l], ln1_b=ln1_b[l], w_up=w_up[l].astype(BF16),
            w_down=w_down[l].astype(BF16), ln2_g=ln2_g[l], ln2_b=ln2_b[l])

        p = _in_stage(hp, tabs_p, wts, None, zero_period=sp // CONV_ROWS, scale=scale)
        kv = _matmul(p["ckvb"], [wts["w_kv"]], [], [BF16], _ep_cast, tn=2048)[0]
        p["attn"] = _attn_prompt(p["q"], kv, p["krb"], batch=bp, seq=sp)
        hp_new = _token_mixer(hp, wts, p, alpha=alpha)
        outs[0].append(p["ckv"].reshape(bp, sp, kv_rank))
        outs[1].append(p["kr"].reshape(bp, sp, ROPE_DIM))
        outs[2].append(p["u"].reshape(bp, sp, conv_dim)[:, sp - hist_len:])

        hist = jnp.pad(state_conv[l], ((0, 0), (HIST_PAD, 0), (0, 0))).reshape(bs * HIST_ROWS, conv_dim)
        s = _in_stage(hs, tabs_s, wts, hist, zero_period=0, scale=scale)
        qa, qr = _absorb_q(s["q"], wts["w_kv"], batch=bs, t=ts)
        cache_k = jnp.pad(cache_krope[l], ((0, 0), (0, 0), (0, ROPE_SLOT - ROPE_DIM))).astype(BF16)
        o_lat = _attn_sample(qa, qr, cache_ckv[l].astype(BF16), cache_k,
                             s["ckvb"].reshape(bs, ts, kv_rank), s["krb"].reshape(bs, ts, ROPE_SLOT))
        s["attn"] = _unabsorb(o_lat, wts["w_kv"], t=ts)
        hs_new = _token_mixer(hs, wts, s, alpha=alpha)
        outs[3].append(s["ckv"].reshape(bs, ts, kv_rank))
        outs[4].append(s["kr"].reshape(bs, ts, ROPE_DIM))
        outs[5].append(s["u"].reshape(bs, ts, conv_dim)[:, ts - hist_len:])
        hp, hs = hp_new, hs_new

    return (hp.reshape(bp, sp, d), hs.reshape(bs, ts, d), jnp.stack(outs[0]), jnp.stack(outs[1]),
            jnp.stack(outs[2]), jnp.stack(outs[3]), jnp.stack(outs[4]), jnp.stack(outs[5]))
```

```python
import functools
import math

import jax
import jax.numpy as jnp
from jax import lax
from jax.experimental import pallas as pl
from jax.experimental.pallas import tpu as pltpu

F32 = jnp.float32
BF16 = jnp.bfloat16

CHUNK = 64
CONV_WIDTH = 31
N_HEADS = 64
NOPE_DIM = 128
ROPE_DIM = 64
V_DIM = 128
QK_DIM = NOPE_DIM + ROPE_DIM
ROPE_THETA = 10000.0
LN_EPS = 1e-5
RMS_EPS = 1e-6
NEG_INF = -1e30

LANES = 128
SUBLANES = 8
VMEM_LIMIT_BYTES = 60 * 1024 * 1024

ROPE_SLOT = LANES
HEAD_SLOT = NOPE_DIM + ROPE_SLOT
HIST_ROWS = 32
HIST_PAD = HIST_ROWS - (CONV_WIDTH - 1)
CONV_ROWS = 64


def _tile(dim, pref):
    t = min(dim, pref)
    assert dim % t == 0, (dim, pref)
    return t


def _params(sem):
    return pltpu.CompilerParams(dimension_semantics=sem, vmem_limit_bytes=VMEM_LIMIT_BYTES)


def _mm_body(*refs, nw, ne, no, nk, epilogue):
    x_ref = refs[0]
    w_refs = refs[1:1 + nw]
    ex = refs[1 + nw:1 + nw + ne]
    outs = refs[1 + nw + ne:1 + nw + ne + no]
    accs = refs[1 + nw + ne + no:]
    parts = [jnp.dot(x_ref[...], w[...], preferred_element_type=F32) for w in w_refs]
    if nk == 1:
        epilogue(parts, ex, outs)
        return
    k = pl.program_id(2)

    @pl.when(k == 0)
    def _():
        for a, p in zip(accs, parts):
            a[...] = p

    @pl.when(jnp.logical_and(k > 0, k < nk - 1))
    def _():
        for a, p in zip(accs, parts):
            a[...] += p

    @pl.when(k == nk - 1)
    def _():
        epilogue([a[...] + p for a, p in zip(accs, parts)], ex, outs)


def _extra_spec(kind, arr, tm, tn, n):
    if kind == "row":
        return pl.BlockSpec((1, tn), lambda i, j, k: (0, j))
    if kind == "rows":
        return pl.BlockSpec((tm, arr.shape[1]), lambda i, j, k: (i, 0))
    off = kind * (n // tn)
    return pl.BlockSpec((tm, tn), lambda i, j, k: (i, j + off))


def _matmul(x, ws, extras, outs, epilogue, *, tm=1024, tn=1024, tk=4096):
    m, kdim = x.shape
    n = ws[0].shape[1]
    tm, tn, tk = _tile(m, tm), _tile(n, tn), _tile(kdim, tk)
    nk = kdim // tk
    in_specs = [pl.BlockSpec((tm, tk), lambda i, j, k: (i, k))]
    in_specs += [pl.BlockSpec((tk, tn), lambda i, j, k: (k, j)) for _ in ws]
    in_specs += [_extra_spec(kind, arr, tm, tn, n) for kind, arr in extras]
    out_specs = [pl.BlockSpec((tm, tn), lambda i, j, k: (i, j)) for _ in outs]
    out_shape = [jax.ShapeDtypeStruct((m, n), dt) for dt in outs]
    scratch = [pltpu.VMEM((tm, tn), F32) for _ in ws] if nk > 1 else []
    body = functools.partial(_mm_body, nw=len(ws), ne=len(extras), no=len(outs), nk=nk,
                             epilogue=epilogue)
    return pl.pallas_call(
        body,
        grid=(m // tm, n // tn, nk),
        in_specs=in_specs,
        out_specs=out_specs,
        out_shape=out_shape,
        scratch_shapes=scratch,
        compiler_params=_params(("parallel", "parallel", "arbitrary")),
        name="mm" + getattr(epilogue, "func", epilogue).__name__,
    )(x, *ws, *[arr for _, arr in extras])


def _row(vec):
    return ("row", vec.reshape(1, -1))


def _ep_glu(parts, ex, outs):
    a = parts[0] + ex[0][...]
    b = parts[1] + ex[1][...]
    outs[0][...] = a * jax.nn.sigmoid(b)


def _ep_sigmoid_bias(parts, ex, outs):
    outs[0][...] = jax.nn.sigmoid(parts[0] + ex[0][...])


def _rope_slot(a, cos, sin_lo, sin_hi):
    half = ROPE_DIM // 2
    return (a * cos + pltpu.roll(a, ROPE_SLOT - half, axis=1) * sin_lo
            + pltpu.roll(a, half, axis=1) * sin_hi)


def _ep_q(parts, ex, outs, *, scale):
    q = parts[0]
    cos, sin_lo, sin_hi = ex[0][...], ex[1][...], ex[2][...]
    for s in range(q.shape[1] // HEAD_SLOT):
        c0 = s * HEAD_SLOT
        outs[0][:, c0:c0 + NOPE_DIM] = (q[:, c0:c0 + NOPE_DIM] * scale).astype(BF16)
        r = _rope_slot(q[:, c0 + NOPE_DIM:c0 + HEAD_SLOT], cos, sin_lo, sin_hi)
        outs[0][:, c0 + NOPE_DIM:c0 + HEAD_SLOT] = (r * scale).astype(BF16)


def _ep_cast(parts, ex, outs):
    outs[0][...] = parts[0].astype(outs[0].dtype)


def _ep_gate(parts, ex, outs):
    outs[0][...] = ex[0][...] * parts[0]


def _ep_merge(parts, ex, outs):
    outs[0][...] = (ex[0][...] * parts[0] + ex[1][...]).astype(BF16)


def _ep_residual(parts, ex, outs, *, alpha):
    outs[0][...] = alpha * ex[0][...] + parts[0]


def _ep_relu2(parts, ex, outs):
    r = jnp.maximum(parts[0], 0.0)
    outs[0][...] = (r * r).astype(BF16)


def _rms(x, g):
    return x * lax.rsqrt(jnp.mean(x * x, axis=-1, keepdims=True) + RMS_EPS) * g


def _latent_body(x_ref, w_ref, b_ref, qg_ref, kvg_ref, cos_ref, slo_ref, shi_ref,
                 qn_ref, ckv_ref, ckvb_ref, kr_ref, krb_ref, *, q_rank, kv_rank):
    acc = jnp.dot(x_ref[...], w_ref[...], preferred_element_type=F32) + b_ref[...]
    qn_ref[...] = _rms(acc[:, :q_rank], qg_ref[...]).astype(BF16)
    ckv = _rms(acc[:, q_rank:q_rank + kv_rank], kvg_ref[...])
    ckv_ref[...] = ckv
    ckvb_ref[...] = ckv.astype(BF16)
    kr = _rope_slot(acc[:, q_rank + kv_rank:], cos_ref[...], slo_ref[...], shi_ref[...])
    kr_ref[...] = kr[:, :ROPE_DIM]
    krb_ref[...] = kr.astype(BF16)


def _latents(xb, w_lat, b_lat, q_g, kv_g, tabs, *, tm=512):
    m, d = xb.shape
    q_rank, kv_rank = q_g.shape[0], kv_g.shape[0]
    n = w_lat.shape[1]
    tm = _tile(m, tm)
    row = lambda i: (i, 0)
    fix = lambda i: (0, 0)
    body = functools.partial(_latent_body, q_rank=q_rank, kv_rank=kv_rank)
    return pl.pallas_call(
        body,
        grid=(m // tm,),
        in_specs=[pl.BlockSpec((tm, d), row), pl.BlockSpec((d, n), fix),
                  pl.BlockSpec((1, n), fix), pl.BlockSpec((1, q_rank), fix),
                  pl.BlockSpec((1, kv_rank), fix)]
                 + [pl.BlockSpec((tm, ROPE_SLOT), row)] * 3,
        out_specs=[pl.BlockSpec((tm, q_rank), row), pl.BlockSpec((tm, kv_rank), row),
                   pl.BlockSpec((tm, kv_rank), row), pl.BlockSpec((tm, ROPE_DIM), row),
                   pl.BlockSpec((tm, ROPE_SLOT), row)],
        out_shape=[jax.ShapeDtypeStruct((m, q_rank), BF16),
                   jax.ShapeDtypeStruct((m, kv_rank), F32),
                   jax.ShapeDtypeStruct((m, kv_rank), BF16),
                   jax.ShapeDtypeStruct((m, ROPE_DIM), F32),
                   jax.ShapeDtypeStruct((m, ROPE_SLOT), BF16)],
        compiler_params=_params(("parallel",)),
        name="latents",
    )(xb, w_lat, b_lat.reshape(1, n), q_g.reshape(1, -1), kv_g.reshape(1, -1), *tabs)


def _conv_body(hist_ref, u_ref, w_ref, bdw_ref, g_ref, b_ref, o_ref, win_ref, y_ref,
               *, zero_period, lane_chunk):
    c = u_ref.shape[1]
    hist = hist_ref[...]
    if zero_period:
        first = (pl.program_id(0) % zero_period) == 0
        hist = jnp.where(first, 0.0, hist)
    win_ref[0:HIST_ROWS, :] = hist
    win_ref[HIST_ROWS:, :] = u_ref[...]
    for c0 in range(0, c, lane_chunk):
        lanes = slice(c0, c0 + lane_chunk)
        acc = bdw_ref[:, lanes]
        for r in range(SUBLANES):
            rows = CONV_ROWS if r == 0 else CONV_ROWS + SUBLANES
            z = None
            for a in range((HIST_PAD + CONV_WIDTH - 1 - r) // SUBLANES + 1):
                k = a * SUBLANES + r - HIST_PAD
                if k < 0:
                    continue
                term = win_ref[a * SUBLANES:a * SUBLANES + rows, lanes] * w_ref[k:k + 1, lanes]
                z = term if z is None else z + term
            acc = acc + z[r:r + CONV_ROWS]
        y_ref[:, lanes] = acc
    y = y_ref[...]
    mu = jnp.mean(y, axis=-1, keepdims=True)
    yc = y - mu
    var = jnp.mean(yc * yc, axis=-1, keepdims=True)
    z = yc * lax.rsqrt(var + LN_EPS) * g_ref[...] + b_ref[...]
    o_ref[...] = (z * jax.nn.sigmoid(z)).astype(BF16)


def _conv(u, hist, w_dw, b_dw, g, b, *, zero_period):
    m, c = u.shape
    w_pad = jnp.pad(w_dw, ((0, HIST_ROWS - CONV_WIDTH), (0, 0)))
    per = CONV_ROWS // HIST_ROWS
    if zero_period:
        hist_map = lambda i: (jnp.maximum(i * per - 1, 0), 0)
    else:
        hist_map = lambda i: (i, 0)
    fix = lambda i: (0, 0)
    body = functools.partial(_conv_body, zero_period=zero_period, lane_chunk=min(c, 2 * LANES))
    return pl.pallas_call(
        body,
        grid=(m // CONV_ROWS,),
        in_specs=[pl.BlockSpec((HIST_ROWS, c), hist_map),
                  pl.BlockSpec((CONV_ROWS, c), lambda i: (i, 0)),
                  pl.BlockSpec((HIST_ROWS, c), fix),
                  pl.BlockSpec((1, c), fix), pl.BlockSpec((1, c), fix), pl.BlockSpec((1, c), fix)],
        out_specs=pl.BlockSpec((CONV_ROWS, c), lambda i: (i, 0)),
        out_shape=jax.ShapeDtypeStruct((m, c), BF16),
        scratch_shapes=[pltpu.VMEM((HIST_ROWS + CONV_ROWS, c), F32),
                        pltpu.VMEM((CONV_ROWS, c), F32)],
        compiler_params=_params(("parallel",)),
        name="conv",
    )(hist, u, w_pad, b_dw.reshape(1, c), g.reshape(1, c), b.reshape(1, c))


def _ln_body(x_ref, g_ref, b_ref, *o_refs):
    x = x_ref[...]
    mu = jnp.mean(x, axis=-1, keepdims=True)
    xc = x - mu
    var = jnp.mean(xc * xc, axis=-1, keepdims=True)
    y = xc * lax.rsqrt(var + LN_EPS) * g_ref[...] + b_ref[...]
    for o in o_refs:
        o[...] = y.astype(o.dtype)


def _layer_norm(x, g, b, dtypes, *, tm=256):
    m, d = x.shape
    tm = _tile(m, tm)
    row = lambda i: (i, 0)
    fix = lambda i: (0, 0)
    return pl.pallas_call(
        _ln_body,
        grid=(m // tm,),
        in_specs=[pl.BlockSpec((tm, d), row), pl.BlockSpec((1, d), fix), pl.BlockSpec((1, d), fix)],
        out_specs=[pl.BlockSpec((tm, d), row) for _ in dtypes],
        out_shape=[jax.ShapeDtypeStruct((m, d), dt) for dt in dtypes],
        compiler_params=_params(("parallel",)),
        name="layer_norm",
    )(x, g.reshape(1, d), b.reshape(1, d))


def _row_reduce(x, combine, reduce):
    part = x[:, :LANES]
    for c in range(LANES, x.shape[1], LANES):
        part = combine(part, x[:, c:c + LANES])
    return reduce(part, axis=1, keepdims=True)


def _attn_prompt_body(q_ref, kn_ref, kr_ref, v_ref, o_ref, kf_ref, vt_ref, qt_ref, s0_ref, s1_ref,
                      p0_ref, p1_ref, acc0_ref, acc1_ref, *, tq):
    seq = q_ref.shape[0]
    kf_ref[:, :NOPE_DIM] = kn_ref[...]
    kf_ref[:, NOPE_DIM:] = kr_ref[...]
    for t in range(seq // tq):
        vt_ref[t] = v_ref[t * tq:(t + 1) * tq, :].T
        qt_ref[t] = q_ref[t * tq:(t + 1) * tq, :].T
    key_chunk = lax.broadcasted_iota(jnp.int32, (tq, tq), 0) // CHUNK
    q_chunk = lax.broadcasted_iota(jnp.int32, (tq, tq), 1) // CHUNK
    visible = key_chunk <= q_chunk

    def scores(qi, kj):
        return jnp.dot(kf_ref[kj * tq:(kj + 1) * tq, :], qt_ref[qi], preferred_element_type=F32)

    def over_keys(x, combine, reduce):
        part = x[0:tq // SUBLANES]
        for g in range(1, SUBLANES):
            part = combine(part, x[g * tq // SUBLANES:(g + 1) * tq // SUBLANES])
        return reduce(part, axis=0, keepdims=True)

    s_refs, p_refs, acc_refs = (s0_ref, s1_ref), (p0_ref, p1_ref), (acc0_ref, acc1_ref)
    step_no = 0
    for qi in range(seq // tq):
        acc_ref = acc_refs[qi % 2]
        m = jnp.full((1, tq), NEG_INF, F32)
        l = jnp.zeros((1, tq), F32)
        acc = a_prev = None
        s_refs[step_no % 2][...] = scores(qi, 0)
        for j in range(qi + 1):
            cur, nxt = step_no % 2, (step_no + 1) % 2
            last = j == qi
            if not last:
                s_refs[nxt][...] = scores(qi, j + 1)
            if j > 0:
                pv_prev = jnp.dot(vt_ref[j - 1], p_refs[nxt][...], preferred_element_type=F32)
                acc = pv_prev if j == 1 else acc_ref[...] * a_prev + pv_prev
            s = s_refs[cur][...]
            if last:
                s = jnp.where(visible, s, NEG_INF)
            m_new = jnp.maximum(m, over_keys(s, jnp.maximum, jnp.max))
            alpha = jnp.exp2(m - m_new)
            p = jnp.exp2(s - m_new)
            l = alpha * l + over_keys(p, jnp.add, jnp.sum)
            m = m_new
            p_refs[cur][...] = p.astype(BF16)
            if last:
                pv_last = jnp.dot(vt_ref[j], p_refs[cur][...], preferred_element_type=F32)
                acc = pv_last if acc is None else acc * alpha + pv_last
                o_ref[qi * tq:(qi + 1) * tq, :] = (acc / l).T.astype(o_ref.dtype)
            else:
                if acc is not None:
                    acc_ref[...] = acc
                a_prev = alpha
            step_no += 1


def _attn_prompt(q, kv, kr, *, batch, seq, tq=512):
    tq = _tile(seq, tq)
    assert tq % CHUNK == 0
    per_head = (NOPE_DIM + V_DIM) // LANES
    body = functools.partial(_attn_prompt_body, tq=tq)
    return pl.pallas_call(
        body,
        grid=(batch, N_HEADS),
        in_specs=[pl.BlockSpec((seq, HEAD_SLOT), lambda b, h: (b, h)),
                  pl.BlockSpec((seq, NOPE_DIM), lambda b, h: (b, per_head * h)),
                  pl.BlockSpec((seq, ROPE_SLOT), lambda b, h: (b, 0)),
                  pl.BlockSpec((seq, V_DIM), lambda b, h: (b, per_head * h + 1))],
        out_specs=pl.BlockSpec((seq, V_DIM), lambda b, h: (b, h)),
        out_shape=jax.ShapeDtypeStruct((batch * seq, N_HEADS * V_DIM), BF16),
        scratch_shapes=[pltpu.VMEM((seq, HEAD_SLOT), BF16),
                        pltpu.VMEM((seq // tq, V_DIM, tq), BF16),
                        pltpu.VMEM((seq // tq, HEAD_SLOT, tq), BF16),
                        pltpu.VMEM((tq, tq), F32), pltpu.VMEM((tq, tq), F32),
                        pltpu.VMEM((tq, tq), BF16), pltpu.VMEM((tq, tq), BF16),
                        pltpu.VMEM((V_DIM, tq), F32), pltpu.VMEM((V_DIM, tq), F32)],
        compiler_params=_params(("parallel", "parallel")),
        name="attn_prompt",
    )(q, kv, kr, kv)


def _absorb_q_body(qn_ref, qr_ref, wuk_ref, qa_ref, qro_ref, *, batch, t):
    qa = lax.dot_general(qn_ref[...], wuk_ref[...], (((1,), (1,)), ((), ())),
                         preferred_element_type=F32).astype(BF16)
    for b in range(batch):
        qa_ref[b] = qa[b * t:(b + 1) * t]
        qro_ref[b] = qr_ref[b * t:(b + 1) * t, :]


def _absorb_q(q, w_kv, *, batch, t):
    m = batch * t
    kv_rank = w_kv.shape[0]
    per_head = (NOPE_DIM + V_DIM) // LANES
    q_per_head = HEAD_SLOT // LANES
    body = functools.partial(_absorb_q_body, batch=batch, t=t)
    return pl.pallas_call(
        body,
        grid=(N_HEADS,),
        in_specs=[pl.BlockSpec((m, NOPE_DIM), lambda h: (0, q_per_head * h)),
                  pl.BlockSpec((m, ROPE_SLOT), lambda h: (0, q_per_head * h + 1)),
                  pl.BlockSpec((kv_rank, NOPE_DIM), lambda h: (0, per_head * h))],
        out_specs=[pl.BlockSpec((batch, t, kv_rank), lambda h: (0, h, 0)),
                   pl.BlockSpec((batch, t, ROPE_SLOT), lambda h: (0, h, 0))],
        out_shape=[jax.ShapeDtypeStruct((batch, N_HEADS * t, kv_rank), BF16),
                   jax.ShapeDtypeStruct((batch, N_HEADS * t, ROPE_SLOT), BF16)],
        compiler_params=_params(("parallel",)),
        name="absorb_q",
    )(q, q, w_kv)


def _attn_sample_body(qa_ref, qr_ref, cc_ref, ck_ref, nc_ref, nk_ref, o_ref):
    qa, qr = qa_ref[0], qr_ref[0]
    cc, ck, nc, nk = cc_ref[0], ck_ref[0], nc_ref[0], nk_ref[0]
    dims = (((1,), (1,)), ((), ()))
    s_old = (lax.dot_general(qa, cc, dims, preferred_element_type=F32)
             + lax.dot_general(qr, ck, dims, preferred_element_type=F32))
    s_new = (lax.dot_general(qa, nc, dims, preferred_element_type=F32)
             + lax.dot_general(qr, nk, dims, preferred_element_type=F32))
    m = jnp.maximum(_row_reduce(s_old, jnp.maximum, jnp.max),
                    jnp.max(s_new, axis=1, keepdims=True))
    p_old = jnp.exp2(s_old - m)
    p_new = jnp.exp2(s_new - m)
    l = _row_reduce(p_old, jnp.add, jnp.sum) + jnp.sum(p_new, axis=1, keepdims=True)
    o = (jnp.dot(p_old.astype(BF16), cc, preferred_element_type=F32)
         + jnp.dot(p_new.astype(BF16), nc, preferred_element_type=F32))
    o_ref[0] = (o / l).astype(BF16)


def _attn_sample(qa, qr, cache_c, cache_k, new_c, new_k, *, tr=512):
    batch, rows, kv_rank = qa.shape
    past, t = cache_c.shape[1], new_c.shape[1]
    tr = _tile(rows, tr)
    return pl.pallas_call(
        _attn_sample_body,
        grid=(batch, rows // tr),
        in_specs=[pl.BlockSpec((1, tr, kv_rank), lambda b, r: (b, r, 0)),
                  pl.BlockSpec((1, tr, ROPE_SLOT), lambda b, r: (b, r, 0)),
                  pl.BlockSpec((1, past, kv_rank), lambda b, r: (b, 0, 0)),
                  pl.BlockSpec((1, past, ROPE_SLOT), lambda b, r: (b, 0, 0)),
                  pl.BlockSpec((1, t, kv_rank), lambda b, r: (b, 0, 0)),
                  pl.BlockSpec((1, t, ROPE_SLOT), lambda b, r: (b, 0, 0))],
        out_specs=pl.BlockSpec((1, tr, kv_rank), lambda b, r: (b, r, 0)),
        out_shape=jax.ShapeDtypeStruct((batch, rows, kv_rank), BF16),
        compiler_params=_params(("parallel", "parallel")),
        name="attn_sample",
    )(qa, qr, cache_c, cache_k, new_c, new_k)


def _unabsorb_body(o_ref, wuv_ref, out_ref):
    b, t, r = o_ref.shape
    out_ref[...] = jnp.dot(o_ref[...].reshape(b * t, r), wuv_ref[...],
                           preferred_element_type=F32).astype(BF16)


def _unabsorb(o_lat, w_kv, *, t):
    batch, _, kv_rank = o_lat.shape
    per_head = (NOPE_DIM + V_DIM) // LANES
    return pl.pallas_call(
        _unabsorb_body,
        grid=(N_HEADS,),
        in_specs=[pl.BlockSpec((batch, t, kv_rank), lambda h: (0, h, 0)),
                  pl.BlockSpec((kv_rank, V_DIM), lambda h: (0, per_head * h + 1))],
        out_specs=pl.BlockSpec((batch * t, V_DIM), lambda h: (0, h)),
        out_shape=jax.ShapeDtypeStruct((batch * t, N_HEADS * V_DIM), BF16),
        compiler_params=_params(("parallel",)),
        name="unabsorb",
    )(o_lat, w_kv)


def _rope_tables(pos):
    half = ROPE_DIM // 2
    inv = ROPE_THETA ** (-jnp.arange(half, dtype=F32) / half)
    ang = pos.astype(F32)[:, None] * inv[None, :]
    cos, sin = jnp.cos(ang), jnp.sin(ang)
    zero = jnp.zeros_like(cos)
    cos_t = jnp.concatenate([cos, cos, zero, zero], axis=1)
    sin_lo = jnp.concatenate([-sin, zero, zero, zero], axis=1)
    sin_hi = jnp.concatenate([zero, sin, zero, zero], axis=1)
    return cos_t, sin_lo, sin_hi


def _prep_weights(l, w_in, b_in, w_q_b, w_kv_b, d, conv_dim, q_rank, kv_rank):
    w, b = w_in[l], b_in[l]
    o_q = 2 * conv_dim
    o_kv = o_q + q_rank
    o_kr = o_kv + kv_rank
    o_g = o_kr + ROPE_DIM
    pad = ROPE_SLOT - ROPE_DIM
    w_lat = jnp.concatenate([w[:, o_q:o_g], jnp.zeros((d, pad), w.dtype)], axis=1).astype(BF16)
    b_lat = jnp.concatenate([b[o_q:o_g], jnp.zeros((pad,), b.dtype)])
    wq = w_q_b[l].reshape(q_rank, N_HEADS, QK_DIM)
    wq = jnp.pad(wq, ((0, 0), (0, 0), (0, HEAD_SLOT - QK_DIM))).reshape(q_rank, N_HEADS * HEAD_SLOT)
    return dict(
        w_ga=w[:, :conv_dim].astype(BF16), b_ga=b[:conv_dim],
        w_gb=w[:, conv_dim:o_q].astype(BF16), b_gb=b[conv_dim:o_q],
        w_lat=w_lat, b_lat=b_lat,
        w_gate=w[:, o_g:].astype(BF16), b_gate=b[o_g:],
        w_q=wq.astype(BF16), w_kv=w_kv_b[l].astype(BF16))


def _token_mixer(x, wts, p, *, alpha):
    residual = functools.partial(_ep_residual, alpha=alpha)
    gconv = _matmul(p["conv_act"], [wts["w_pw"]], [(0, p["gates"])], [F32], _ep_gate)[0]
    merged = _matmul(p["attn"], [wts["w_o"]], [(1, p["gates"]), (0, gconv)], [BF16], _ep_merge,
                     tm=512, tn=512, tk=8192)[0]
    pre1 = _matmul(merged, [wts["w_out"]], [(0, x)], [F32], residual)[0]
    h, hb = _layer_norm(pre1, wts["ln1_g"], wts["ln1_b"], [F32, BF16])
    act = _matmul(hb, [wts["w_up"]], [], [BF16], _ep_relu2)[0]
    pre2 = _matmul(act, [wts["w_down"]], [(0, h)], [F32], residual, tk=2048)[0]
    return _layer_norm(pre2, wts["ln2_g"], wts["ln2_b"], [F32])[0]


def _in_stage(x, tabs, wts, hist, *, zero_period, scale):
    xb = x.astype(BF16)
    u = _matmul(xb, [wts["w_ga"], wts["w_gb"]], [_row(wts["b_ga"]), _row(wts["b_gb"])],
                [F32], _ep_glu, tn=512)[0]
    gates = _matmul(xb, [wts["w_gate"]], [_row(wts["b_gate"])], [F32], _ep_sigmoid_bias)[0]
    qn, ckv, ckvb, kr, krb = _latents(xb, wts["w_lat"], wts["b_lat"], wts["q_a_g"],
                                      wts["kv_a_g"], tabs)
    conv_act = _conv(u, u if zero_period else hist, wts["w_dw"], wts["b_dw"],
                     wts["conv_ln_g"], wts["conv_ln_b"], zero_period=zero_period)
    q = _matmul(qn, [wts["w_q"]], [("rows", t) for t in tabs], [BF16],
                functools.partial(_ep_q, scale=scale))[0]
    return dict(u=u, gates=gates, ckv=ckv, ckvb=ckvb, kr=kr, krb=krb, conv_act=conv_act, q=q)


def kernel(x_prompt, x_sample, cache_ckv, cache_krope, state_conv, w_in, b_in, w_dw, b_dw,
           conv_ln_g, conv_ln_b, w_conv_pw, q_a_g, w_q_b, kv_a_g, w_kv_b, w_attn_o, w_out,
           ln1_g, ln1_b, w_up, w_down, ln2_g, ln2_b):
    depth = w_in.shape[0]
    bp, sp, d = x_prompt.shape
    bs, ts, _ = x_sample.shape
    past = cache_ckv.shape[2]
    conv_dim = w_dw.shape[2]
    q_rank, kv_rank = q_a_g.shape[1], kv_a_g.shape[1]
    alpha = (2 * depth) ** 0.25
    scale = QK_DIM ** -0.5 * math.log2(math.e)
    hist_len = CONV_WIDTH - 1
    assert ts == CONV_ROWS and sp % CONV_ROWS == 0

    tabs_p = [jnp.tile(t, (bp, 1)) for t in _rope_tables(jnp.arange(sp))]
    tabs_s = [jnp.tile(t, (bs, 1)) for t in _rope_tables(past + jnp.arange(ts))]

    hp = x_prompt.reshape(bp * sp, d)
    hs = x_sample.reshape(bs * ts, d)
    outs = [[] for _ in range(6)]
    for l in range(depth):
        wts = _prep_weights(l, w_in, b_in, w_q_b, w_kv_b, d, conv_dim, q_rank, kv_rank)
        wts.update(
            w_dw=w_dw[l], b_dw=b_dw[l], conv_ln_g=conv_ln_g[l], conv_ln_b=conv_ln_b[l],
            q_a_g=q_a_g[l], kv_a_g=kv_a_g[l], w_pw=w_conv_pw[l].astype(BF16),
            w_o=w_attn_o[l].astype(BF16), w_out=w_out[l].astype(BF16),
            ln1_g=ln1_g[l], ln1_b=ln1_b[l], w_up=w_up[l].astype(BF16),
            w_down=w_down[l].astype(BF16), ln2_g=ln2_g[l], ln2_b=ln2_b[l])

        p = _in_stage(hp, tabs_p, wts, None, zero_period=sp // CONV_ROWS, scale=scale)
        kv = _matmul(p["ckvb"], [wts["w_kv"]], [], [BF16], _ep_cast, tn=2048)[0]
        p["attn"] = _attn_prompt(p["q"], kv, p["krb"], batch=bp, seq=sp)
        hp_new = _token_mixer(hp, wts, p, alpha=alpha)
        outs[0].append(p["ckv"].reshape(bp, sp, kv_rank))
        outs[1].append(p["kr"].reshape(bp, sp, ROPE_DIM))
        outs[2].append(p["u"].reshape(bp, sp, conv_dim)[:, sp - hist_len:])

        hist = jnp.pad(state_conv[l], ((0, 0), (HIST_PAD, 0), (0, 0))).reshape(bs * HIST_ROWS, conv_dim)
        s = _in_stage(hs, tabs_s, wts, hist, zero_period=0, scale=scale)
        qa, qr = _absorb_q(s["q"], wts["w_kv"], batch=bs, t=ts)
        cache_k = jnp.pad(cache_krope[l], ((0, 0), (0, 0), (0, ROPE_SLOT - ROPE_DIM))).astype(BF16)
        o_lat = _attn_sample(qa, qr, cache_ckv[l].astype(BF16), cache_k,
                             s["ckvb"].reshape(bs, ts, kv_rank), s["krb"].reshape(bs, ts, ROPE_SLOT))
        s["attn"] = _unabsorb(o_lat, wts["w_kv"], t=ts)
        hs_new = _token_mixer(hs, wts, s, alpha=alpha)
        outs[3].append(s["ckv"].reshape(bs, ts, kv_rank))
        outs[4].append(s["kr"].reshape(bs, ts, ROPE_DIM))
        outs[5].append(s["u"].reshape(bs, ts, conv_dim)[:, ts - hist_len:])
        hp, hs = hp_new, hs_new

    return (hp.reshape(bp, sp, d), hs.reshape(bs, ts, d), jnp.stack(outs[0]), jnp.stack(outs[1]),
            jnp.stack(outs[2]), jnp.stack(outs[3]), jnp.stack(outs[4]), jnp.stack(outs[5]))
```

```python
import functools
import math

import jax
import jax.numpy as jnp
from jax import lax
from jax.experimental import pallas as pl
from jax.experimental.pallas import tpu as pltpu

F32 = jnp.float32
BF16 = jnp.bfloat16

CHUNK = 64
CONV_WIDTH = 31
N_HEADS = 64
NOPE_DIM = 128
ROPE_DIM = 64
V_DIM = 128
QK_DIM = NOPE_DIM + ROPE_DIM
ROPE_THETA = 10000.0
LN_EPS = 1e-5
RMS_EPS = 1e-6
NEG_INF = -1e30

LANES = 128
SUBLANES = 8
VMEM_LIMIT_BYTES = 60 * 1024 * 1024

ROPE_SLOT = LANES
HEAD_SLOT = NOPE_DIM + ROPE_SLOT
HIST_ROWS = 32
HIST_PAD = HIST_ROWS - (CONV_WIDTH - 1)
CONV_ROWS = 64


def _tile(dim, pref):
    t = min(dim, pref)
    assert dim % t == 0, (dim, pref)
    return t


def _params(sem):
    return pltpu.CompilerParams(dimension_semantics=sem, vmem_limit_bytes=VMEM_LIMIT_BYTES)


def _mm_body(*refs, nw, ne, no, nk, epilogue):
    x_ref = refs[0]
    w_refs = refs[1:1 + nw]
    ex = refs[1 + nw:1 + nw + ne]
    outs = refs[1 + nw + ne:1 + nw + ne + no]
    accs = refs[1 + nw + ne + no:]
    parts = [jnp.dot(x_ref[...], w[...], preferred_element_type=F32) for w in w_refs]
    if nk == 1:
        epilogue(parts, ex, outs)
        return
    k = pl.program_id(2)

    @pl.when(k == 0)
    def _():
        for a, p in zip(accs, parts):
            a[...] = p

    @pl.when(jnp.logical_and(k > 0, k < nk - 1))
    def _():
        for a, p in zip(accs, parts):
            a[...] += p

    @pl.when(k == nk - 1)
    def _():
        epilogue([a[...] + p for a, p in zip(accs, parts)], ex, outs)


def _extra_spec(kind, arr, tm, tn, n, ij):
    if kind == "row":
        return pl.BlockSpec((1, tn), lambda *g: (0, ij(*g)[1]))
    if kind == "rows":
        return pl.BlockSpec((tm, arr.shape[1]), lambda *g: (ij(*g)[0], 0))
    off = kind * (n // tn)
    return pl.BlockSpec((tm, tn), lambda *g: (ij(*g)[0], ij(*g)[1] + off))


def _matmul(x, ws, extras, outs, epilogue, *, tm=1024, tn=1024, tk=4096):
    m, kdim = x.shape
    n = ws[0].shape[1]
    tm, tn, tk = _tile(m, tm), _tile(n, tn), _tile(kdim, tk)
    nk = kdim // tk
    in_specs = [pl.BlockSpec((tm, tk), lambda i, j, k: (i, k))]
    in_specs += [pl.BlockSpec((tk, tn), lambda i, j, k: (k, j)) for _ in ws]
    in_specs += [_extra_spec(kind, arr, tm, tn, n, lambda i, j, k: (i, j)) for kind, arr in extras]
    out_specs = [pl.BlockSpec((tm, tn), lambda i, j, k: (i, j)) for _ in outs]
    out_shape = [jax.ShapeDtypeStruct((m, n), dt) for dt in outs]
    scratch = [pltpu.VMEM((tm, tn), F32) for _ in ws] if nk > 1 else []
    body = functools.partial(_mm_body, nw=len(ws), ne=len(extras), no=len(outs), nk=nk,
                             epilogue=epilogue)
    return pl.pallas_call(
        body,
        grid=(m // tm, n // tn, nk),
        in_specs=in_specs,
        out_specs=out_specs,
        out_shape=out_shape,
        scratch_shapes=scratch,
        compiler_params=_params(("parallel", "parallel", "arbitrary")),
        name="mm" + getattr(epilogue, "func", epilogue).__name__,
    )(x, *ws, *[arr for _, arr in extras])


def _row(vec):
    return ("row", vec.reshape(1, -1))


def _cast_weight(w_ref, wb_ref):
    wb_ref[...] = w_ref[...].astype(BF16)


def _pad_q_weight(w_ref, wb_ref):
    for h in range(wb_ref.shape[1] // HEAD_SLOT):
        c0 = h * HEAD_SLOT
        wb_ref[:, c0:c0 + QK_DIM] = w_ref[:, h * QK_DIM:(h + 1) * QK_DIM].astype(BF16)
        wb_ref[:, c0 + QK_DIM:c0 + HEAD_SLOT] = jnp.zeros(
            (wb_ref.shape[0], HEAD_SLOT - QK_DIM), BF16)


def _mmw_body(*refs, nw, ne, no, epilogue, prep, transposed):
    x_ref = refs[0]
    w_refs = refs[1:1 + nw]
    ex = refs[1 + nw:1 + nw + ne]
    outs = refs[1 + nw + ne:1 + nw + ne + no]
    wb_refs = refs[1 + nw + ne + no:]

    @pl.when(pl.program_id(1) == 0)
    def _():
        for w, wb in zip(w_refs, wb_refs):
            prep(w, wb)

    dims = (((1,), (1 if transposed else 0,)), ((), ()))
    parts = [lax.dot_general(x_ref[...], wb[...], dims, preferred_element_type=F32)
             for wb in wb_refs]
    epilogue(parts, ex, outs)


def _matmul_f32w(x, ws, extras, outs, epilogue, *, n, tm=1024, tn=512, w_block=None,
                 prep=_cast_weight, transposed=False):
    m, kdim = x.shape
    tm, tn = _tile(m, tm), _tile(n, tn)
    w_block = w_block or tn
    ij = lambda j, i: (i, j)
    in_specs = [pl.BlockSpec((tm, kdim), lambda j, i: (i, 0))]
    for _, col0 in ws:
        assert col0 % w_block == 0
        in_specs.append(pl.BlockSpec(
            (w_block, kdim) if transposed else (kdim, w_block),
            functools.partial(_w_index, col0 // w_block, transposed)))
    in_specs += [_extra_spec(kind, arr, tm, tn, n, ij) for kind, arr in extras]
    body = functools.partial(_mmw_body, nw=len(ws), ne=len(extras), no=len(outs),
                             epilogue=epilogue, prep=prep, transposed=transposed)
    return pl.pallas_call(
        body,
        grid=(n // tn, m // tm),
        in_specs=in_specs,
        out_specs=[pl.BlockSpec((tm, tn), lambda j, i: (i, j)) for _ in outs],
        out_shape=[jax.ShapeDtypeStruct((m, n), dt) for dt in outs],
        scratch_shapes=[pltpu.VMEM((tn, kdim) if transposed else (kdim, tn), BF16) for _ in ws],
        compiler_params=_params(("parallel", "arbitrary")),
        name="mmw" + getattr(epilogue, "func", epilogue).__name__,
    )(x, *[w for w, _ in ws], *[arr for _, arr in extras])


def _w_index(off, transposed, j, i):
    return (j + off, 0) if transposed else (0, j + off)


def _ep_glu(parts, ex, outs):
    a = parts[0] + ex[0][...]
    b = parts[1] + ex[1][...]
    outs[0][...] = a * jax.nn.sigmoid(b)


def _ep_sigmoid_bias(parts, ex, outs):
    outs[0][...] = jax.nn.sigmoid(parts[0] + ex[0][...])


def _rope_slot(a, cos, sin_lo, sin_hi):
    half = ROPE_DIM // 2
    return (a * cos + pltpu.roll(a, ROPE_SLOT - half, axis=1) * sin_lo
            + pltpu.roll(a, half, axis=1) * sin_hi)


def _ep_q(parts, ex, outs, *, scale):
    q = parts[0]
    cos, sin_lo, sin_hi = ex[0][...], ex[1][...], ex[2][...]
    for s in range(q.shape[1] // HEAD_SLOT):
        c0 = s * HEAD_SLOT
        outs[0][:, c0:c0 + NOPE_DIM] = (q[:, c0:c0 + NOPE_DIM] * scale).astype(BF16)
        r = _rope_slot(q[:, c0 + NOPE_DIM:c0 + HEAD_SLOT], cos, sin_lo, sin_hi)
        outs[0][:, c0 + NOPE_DIM:c0 + HEAD_SLOT] = (r * scale).astype(BF16)


def _ep_cast(parts, ex, outs):
    outs[0][...] = parts[0].astype(outs[0].dtype)


def _ep_gate(parts, ex, outs):
    outs[0][...] = ex[0][...] * parts[0]


def _ep_merge(parts, ex, outs):
    outs[0][...] = (ex[0][...] * parts[0] + ex[1][...]).astype(BF16)


def _ep_residual(parts, ex, outs, *, alpha):
    outs[0][...] = alpha * ex[0][...] + parts[0]


def _ep_relu2(parts, ex, outs):
    r = jnp.maximum(parts[0], 0.0)
    outs[0][...] = (r * r).astype(BF16)


def _rms(x, g):
    return x * lax.rsqrt(jnp.mean(x * x, axis=-1, keepdims=True) + RMS_EPS) * g


def _latent_body(x_ref, w_ref, b_ref, qg_ref, kvg_ref, cos_ref, slo_ref, shi_ref,
                 qn_ref, ckv_ref, ckvb_ref, kr_ref, krb_ref, *, q_rank, kv_rank):
    acc = lax.dot_general(x_ref[...], w_ref[...], (((1,), (1,)), ((), ())),
                          preferred_element_type=F32) + b_ref[...]
    qn_ref[...] = _rms(acc[:, :q_rank], qg_ref[...]).astype(BF16)
    ckv = _rms(acc[:, q_rank:q_rank + kv_rank], kvg_ref[...])
    ckv_ref[...] = ckv
    ckvb_ref[...] = ckv.astype(BF16)
    kr = _rope_slot(acc[:, q_rank + kv_rank:], cos_ref[...], slo_ref[...], shi_ref[...])
    kr_ref[...] = kr[:, :ROPE_DIM]
    krb_ref[...] = kr.astype(BF16)


def _latents(xb, w_lat, b_lat, q_g, kv_g, tabs, *, tm=512):
    m, d = xb.shape
    q_rank, kv_rank = q_g.shape[0], kv_g.shape[0]
    n = w_lat.shape[0]
    tm = _tile(m, tm)
    row = lambda i: (i, 0)
    fix = lambda i: (0, 0)
    body = functools.partial(_latent_body, q_rank=q_rank, kv_rank=kv_rank)
    return pl.pallas_call(
        body,
        grid=(m // tm,),
        in_specs=[pl.BlockSpec((tm, d), row), pl.BlockSpec((n, d), fix),
                  pl.BlockSpec((1, n), fix), pl.BlockSpec((1, q_rank), fix),
                  pl.BlockSpec((1, kv_rank), fix)]
                 + [pl.BlockSpec((tm, ROPE_SLOT), row)] * 3,
        out_specs=[pl.BlockSpec((tm, q_rank), row), pl.BlockSpec((tm, kv_rank), row),
                   pl.BlockSpec((tm, kv_rank), row), pl.BlockSpec((tm, ROPE_DIM), row),
                   pl.BlockSpec((tm, ROPE_SLOT), row)],
        out_shape=[jax.ShapeDtypeStruct((m, q_rank), BF16),
                   jax.ShapeDtypeStruct((m, kv_rank), F32),
                   jax.ShapeDtypeStruct((m, kv_rank), BF16),
                   jax.ShapeDtypeStruct((m, ROPE_DIM), F32),
                   jax.ShapeDtypeStruct((m, ROPE_SLOT), BF16)],
        compiler_params=_params(("parallel",)),
        name="latents",
    )(xb, w_lat, b_lat.reshape(1, n), q_g.reshape(1, -1), kv_g.reshape(1, -1), *tabs)


def _conv_body(hist_ref, u_ref, w_ref, bdw_ref, g_ref, b_ref, o_ref, win_ref, y_ref,
               *, zero_period, lane_chunk):
    c = u_ref.shape[1]
    hist = hist_ref[...]
    if zero_period:
        first = (pl.program_id(0) % zero_period) == 0
        hist = jnp.where(first, 0.0, hist)
    win_ref[0:HIST_ROWS, :] = hist
    win_ref[HIST_ROWS:, :] = u_ref[...]
    for c0 in range(0, c, lane_chunk):
        lanes = slice(c0, c0 + lane_chunk)
        acc = bdw_ref[:, lanes]
        for r in range(SUBLANES):
            rows = CONV_ROWS if r == 0 else CONV_ROWS + SUBLANES
            z = None
            for a in range((HIST_PAD + CONV_WIDTH - 1 - r) // SUBLANES + 1):
                k = a * SUBLANES + r - HIST_PAD
                if k < 0:
                    continue
                term = win_ref[a * SUBLANES:a * SUBLANES + rows, lanes] * w_ref[k:k + 1, lanes]
                z = term if z is None else z + term
            acc = acc + z[r:r + CONV_ROWS]
        y_ref[:, lanes] = acc
    y = y_ref[...]
    mu = jnp.mean(y, axis=-1, keepdims=True)
    yc = y - mu
    var = jnp.mean(yc * yc, axis=-1, keepdims=True)
    z = yc * lax.rsqrt(var + LN_EPS) * g_ref[...] + b_ref[...]
    o_ref[...] = (z * jax.nn.sigmoid(z)).astype(BF16)


def _conv(u, hist, w_dw, b_dw, g, b, *, zero_period):
    m, c = u.shape
    w_pad = jnp.pad(w_dw, ((0, HIST_ROWS - CONV_WIDTH), (0, 0)))
    per = CONV_ROWS // HIST_ROWS
    if zero_period:
        hist_map = lambda i: (jnp.maximum(i * per - 1, 0), 0)
    else:
        hist_map = lambda i: (i, 0)
    fix = lambda i: (0, 0)
    body = functools.partial(_conv_body, zero_period=zero_period, lane_chunk=min(c, 2 * LANES))
    return pl.pallas_call(
        body,
        grid=(m // CONV_ROWS,),
        in_specs=[pl.BlockSpec((HIST_ROWS, c), hist_map),
                  pl.BlockSpec((CONV_ROWS, c), lambda i: (i, 0)),
                  pl.BlockSpec((HIST_ROWS, c), fix),
                  pl.BlockSpec((1, c), fix), pl.BlockSpec((1, c), fix), pl.BlockSpec((1, c), fix)],
        out_specs=pl.BlockSpec((CONV_ROWS, c), lambda i: (i, 0)),
        out_shape=jax.ShapeDtypeStruct((m, c), BF16),
        scratch_shapes=[pltpu.VMEM((HIST_ROWS + CONV_ROWS, c), F32),
                        pltpu.VMEM((CONV_ROWS, c), F32)],
        compiler_params=_params(("parallel",)),
        name="conv",
    )(hist, u, w_pad, b_dw.reshape(1, c), g.reshape(1, c), b.reshape(1, c))


def _ln_body(x_ref, g_ref, b_ref, *o_refs):
    x = x_ref[...]
    mu = jnp.mean(x, axis=-1, keepdims=True)
    xc = x - mu
    var = jnp.mean(xc * xc, axis=-1, keepdims=True)
    y = xc * lax.rsqrt(var + LN_EPS) * g_ref[...] + b_ref[...]
    for o in o_refs:
        o[...] = y.astype(o.dtype)


def _layer_norm(x, g, b, dtypes, *, tm=256):
    m, d = x.shape
    tm = _tile(m, tm)
    row = lambda i: (i, 0)
    fix = lambda i: (0, 0)
    return pl.pallas_call(
        _ln_body,
        grid=(m // tm,),
        in_specs=[pl.BlockSpec((tm, d), row), pl.BlockSpec((1, d), fix), pl.BlockSpec((1, d), fix)],
        out_specs=[pl.BlockSpec((tm, d), row) for _ in dtypes],
        out_shape=[jax.ShapeDtypeStruct((m, d), dt) for dt in dtypes],
        compiler_params=_params(("parallel",)),
        name="layer_norm",
    )(x, g.reshape(1, d), b.reshape(1, d))


def _row_reduce(x, combine, reduce):
    part = x[:, :LANES]
    for c in range(LANES, x.shape[1], LANES):
        part = combine(part, x[:, c:c + LANES])
    return reduce(part, axis=1, keepdims=True)


def _attn_prompt_body(q_ref, kn_ref, kr_ref, v_ref, o_ref, kf_ref, vt_ref, qt_ref, s0_ref, s1_ref,
                      p0_ref, p1_ref, acc0_ref, acc1_ref, *, tq):
    seq = q_ref.shape[0]
    kf_ref[:, :NOPE_DIM] = kn_ref[...]
    kf_ref[:, NOPE_DIM:] = kr_ref[...]
    for t in range(seq // tq):
        vt_ref[t] = v_ref[t * tq:(t + 1) * tq, :].T
        qt_ref[t] = q_ref[t * tq:(t + 1) * tq, :].T
    key_chunk = lax.broadcasted_iota(jnp.int32, (tq, tq), 0) // CHUNK
    q_chunk = lax.broadcasted_iota(jnp.int32, (tq, tq), 1) // CHUNK
    visible = key_chunk <= q_chunk

    def scores(qi, kj):
        return jnp.dot(kf_ref[kj * tq:(kj + 1) * tq, :], qt_ref[qi], preferred_element_type=F32)

    def over_keys(x, combine, reduce):
        part = x[0:tq // SUBLANES]
        for g in range(1, SUBLANES):
            part = combine(part, x[g * tq // SUBLANES:(g + 1) * tq // SUBLANES])
        return reduce(part, axis=0, keepdims=True)

    s_refs, p_refs, acc_refs = (s0_ref, s1_ref), (p0_ref, p1_ref), (acc0_ref, acc1_ref)
    step_no = 0
    for qi in range(seq // tq):
        acc_ref = acc_refs[qi % 2]
        m = jnp.full((1, tq), NEG_INF, F32)
        l = jnp.zeros((1, tq), F32)
        acc = a_prev = None
        s_refs[step_no % 2][...] = scores(qi, 0)
        for j in range(qi + 1):
            cur, nxt = step_no % 2, (step_no + 1) % 2
            last = j == qi
            if not last:
                s_refs[nxt][...] = scores(qi, j + 1)
            if j > 0:
                pv_prev = jnp.dot(vt_ref[j - 1], p_refs[nxt][...], preferred_element_type=F32)
                acc = pv_prev if j == 1 else acc_ref[...] * a_prev + pv_prev
            s = s_refs[cur][...]
            if last:
                s = jnp.where(visible, s, NEG_INF)
            m_new = jnp.maximum(m, over_keys(s, jnp.maximum, jnp.max))
            alpha = jnp.exp2(m - m_new)
            p = jnp.exp2(s - m_new)
            l = alpha * l + over_keys(p, jnp.add, jnp.sum)
            m = m_new
            p_refs[cur][...] = p.astype(BF16)
            if last:
                pv_last = jnp.dot(vt_ref[j], p_refs[cur][...], preferred_element_type=F32)
                acc = pv_last if acc is None else acc * alpha + pv_last
                o_ref[qi * tq:(qi + 1) * tq, :] = (acc / l).T.astype(o_ref.dtype)
            else:
                if acc is not None:
                    acc_ref[...] = acc
                a_prev = alpha
            step_no += 1


def _attn_prompt(q, kv, kr, *, batch, seq, tq=512):
    tq = _tile(seq, tq)
    assert tq % CHUNK == 0
    per_head = (NOPE_DIM + V_DIM) // LANES
    body = functools.partial(_attn_prompt_body, tq=tq)
    return pl.pallas_call(
        body,
        grid=(batch, N_HEADS),
        in_specs=[pl.BlockSpec((seq, HEAD_SLOT), lambda b, h: (b, h)),
                  pl.BlockSpec((seq, NOPE_DIM), lambda b, h: (b, per_head * h)),
                  pl.BlockSpec((seq, ROPE_SLOT), lambda b, h: (b, 0)),
                  pl.BlockSpec((seq, V_DIM), lambda b, h: (b, per_head * h + 1))],
        out_specs=pl.BlockSpec((seq, V_DIM), lambda b, h: (b, h)),
        out_shape=jax.ShapeDtypeStruct((batch * seq, N_HEADS * V_DIM), BF16),
        scratch_shapes=[pltpu.VMEM((seq, HEAD_SLOT), BF16),
                        pltpu.VMEM((seq // tq, V_DIM, tq), BF16),
                        pltpu.VMEM((seq // tq, HEAD_SLOT, tq), BF16),
                        pltpu.VMEM((tq, tq), F32), pltpu.VMEM((tq, tq), F32),
                        pltpu.VMEM((tq, tq), BF16), pltpu.VMEM((tq, tq), BF16),
                        pltpu.VMEM((V_DIM, tq), F32), pltpu.VMEM((V_DIM, tq), F32)],
        compiler_params=_params(("parallel", "parallel")),
        name="attn_prompt",
    )(q, kv, kr, kv)


def _absorb_q_body(qn_ref, qr_ref, wuk_ref, qa_ref, qro_ref, *, batch, t):
    qa = lax.dot_general(qn_ref[...], wuk_ref[...].astype(BF16), (((1,), (1,)), ((), ())),
                         preferred_element_type=F32).astype(BF16)
    for b in range(batch):
        qa_ref[b] = qa[b * t:(b + 1) * t]
        qro_ref[b] = qr_ref[b * t:(b + 1) * t, :]


def _absorb_q(q, w_kv, *, batch, t):
    m = batch * t
    kv_rank = w_kv.shape[0]
    per_head = (NOPE_DIM + V_DIM) // LANES
    q_per_head = HEAD_SLOT // LANES
    body = functools.partial(_absorb_q_body, batch=batch, t=t)
    return pl.pallas_call(
        body,
        grid=(N_HEADS,),
        in_specs=[pl.BlockSpec((m, NOPE_DIM), lambda h: (0, q_per_head * h)),
                  pl.BlockSpec((m, ROPE_SLOT), lambda h: (0, q_per_head * h + 1)),
                  pl.BlockSpec((kv_rank, NOPE_DIM), lambda h: (0, per_head * h))],
        out_specs=[pl.BlockSpec((batch, t, kv_rank), lambda h: (0, h, 0)),
                   pl.BlockSpec((batch, t, ROPE_SLOT), lambda h: (0, h, 0))],
        out_shape=[jax.ShapeDtypeStruct((batch, N_HEADS * t, kv_rank), BF16),
                   jax.ShapeDtypeStruct((batch, N_HEADS * t, ROPE_SLOT), BF16)],
        compiler_params=_params(("parallel",)),
        name="absorb_q",
    )(q, q, w_kv)


def _attn_sample_body(qa_ref, qr_ref, cc_ref, ck_ref, nc_ref, nk_ref, o_ref):
    qa, qr = qa_ref[0], qr_ref[0]
    cc, ck, nc, nk = cc_ref[0], ck_ref[0], nc_ref[0], nk_ref[0]
    dims = (((1,), (1,)), ((), ()))
    s_old = (lax.dot_general(qa, cc, dims, preferred_element_type=F32)
             + lax.dot_general(qr, ck, dims, preferred_element_type=F32))
    s_new = (lax.dot_general(qa, nc, dims, preferred_element_type=F32)
             + lax.dot_general(qr, nk, dims, preferred_element_type=F32))
    m = jnp.maximum(_row_reduce(s_old, jnp.maximum, jnp.max),
                    jnp.max(s_new, axis=1, keepdims=True))
    p_old = jnp.exp2(s_old - m)
    p_new = jnp.exp2(s_new - m)
    l = _row_reduce(p_old, jnp.add, jnp.sum) + jnp.sum(p_new, axis=1, keepdims=True)
    o = (jnp.dot(p_old.astype(BF16), cc, preferred_element_type=F32)
         + jnp.dot(p_new.astype(BF16), nc, preferred_element_type=F32))
    o_ref[0] = (o / l).astype(BF16)


def _attn_sample(qa, qr, cache_c, cache_k, new_c, new_k, *, tr=512):
    batch, rows, kv_rank = qa.shape
    past, t = cache_c.shape[1], new_c.shape[1]
    tr = _tile(rows, tr)
    return pl.pallas_call(
        _attn_sample_body,
        grid=(batch, rows // tr),
        in_specs=[pl.BlockSpec((1, tr, kv_rank), lambda b, r: (b, r, 0)),
                  pl.BlockSpec((1, tr, ROPE_SLOT), lambda b, r: (b, r, 0)),
                  pl.BlockSpec((1, past, kv_rank), lambda b, r: (b, 0, 0)),
                  pl.BlockSpec((1, past, ROPE_SLOT), lambda b, r: (b, 0, 0)),
                  pl.BlockSpec((1, t, kv_rank), lambda b, r: (b, 0, 0)),
                  pl.BlockSpec((1, t, ROPE_SLOT), lambda b, r: (b, 0, 0))],
        out_specs=pl.BlockSpec((1, tr, kv_rank), lambda b, r: (b, r, 0)),
        out_shape=jax.ShapeDtypeStruct((batch, rows, kv_rank), BF16),
        compiler_params=_params(("parallel", "parallel")),
        name="attn_sample",
    )(qa, qr, cache_c, cache_k, new_c, new_k)


def _unabsorb_body(o_ref, wuv_ref, out_ref):
    b, t, r = o_ref.shape
    out_ref[...] = jnp.dot(o_ref[...].reshape(b * t, r), wuv_ref[...].astype(BF16),
                           preferred_element_type=F32).astype(BF16)


def _unabsorb(o_lat, w_kv, *, t):
    batch, _, kv_rank = o_lat.shape
    per_head = (NOPE_DIM + V_DIM) // LANES
    return pl.pallas_call(
        _unabsorb_body,
        grid=(N_HEADS,),
        in_specs=[pl.BlockSpec((batch, t, kv_rank), lambda h: (0, h, 0)),
                  pl.BlockSpec((kv_rank, V_DIM), lambda h: (0, per_head * h + 1))],
        out_specs=pl.BlockSpec((batch * t, V_DIM), lambda h: (0, h)),
        out_shape=jax.ShapeDtypeStruct((batch * t, N_HEADS * V_DIM), BF16),
        compiler_params=_params(("parallel",)),
        name="unabsorb",
    )(o_lat, w_kv)


def _rope_tables(pos):
    half = ROPE_DIM // 2
    inv = ROPE_THETA ** (-jnp.arange(half, dtype=F32) / half)
    ang = pos.astype(F32)[:, None] * inv[None, :]
    cos, sin = jnp.cos(ang), jnp.sin(ang)
    zero = jnp.zeros_like(cos)
    cos_t = jnp.concatenate([cos, cos, zero, zero], axis=1)
    sin_lo = jnp.concatenate([-sin, zero, zero, zero], axis=1)
    sin_hi = jnp.concatenate([zero, sin, zero, zero], axis=1)
    return cos_t, sin_lo, sin_hi


def _prep_weights(l, w_in, b_in, w_q_b, w_kv_b, d, conv_dim, q_rank, kv_rank):
    w, b = jnp.swapaxes(w_in[l], 0, 1), b_in[l]
    o_q = 2 * conv_dim
    o_kv = o_q + q_rank
    o_kr = o_kv + kv_rank
    o_g = o_kr + ROPE_DIM
    pad = ROPE_SLOT - ROPE_DIM
    w_lat = jnp.concatenate([w[o_q:o_g], jnp.zeros((pad, d), w.dtype)], axis=0).astype(BF16)
    b_lat = jnp.concatenate([b[o_q:o_g], jnp.zeros((pad,), b.dtype)])
    return dict(
        w_in=w, b_ga=b[:conv_dim], b_gb=b[conv_dim:o_q], w_lat=w_lat, b_lat=b_lat,
        w_gate=w[o_g:], b_gate=b[o_g:], w_q=w_q_b[l], w_kv=w_kv_b[l])


def _token_mixer(x, wts, p, *, alpha):
    residual = functools.partial(_ep_residual, alpha=alpha)
    d = x.shape[1]
    gconv = _matmul_f32w(p["conv_act"], [(wts["w_pw"], 0)], [(0, p["gates"])], [F32], _ep_gate,
                         n=d)[0]
    merged = _matmul(p["attn"], [wts["w_o"]], [(1, p["gates"]), (0, gconv)], [BF16], _ep_merge,
                     tm=1024, tn=256, tk=8192)[0]
    pre1 = _matmul_f32w(merged, [(wts["w_out"], 0)], [(0, x)], [F32], residual, n=d)[0]
    h, hb = _layer_norm(pre1, wts["ln1_g"], wts["ln1_b"], [F32, BF16])
    act = _matmul_f32w(hb, [(wts["w_up"], 0)], [], [BF16], _ep_relu2,
                       n=wts["w_up"].shape[1])[0]
    pre2 = _matmul(act, [wts["w_down"]], [(0, h)], [F32], residual, tk=2048)[0]
    return _layer_norm(pre2, wts["ln2_g"], wts["ln2_b"], [F32])[0]


def _in_stage(x, tabs, wts, hist, *, zero_period, scale):
    xb = x.astype(BF16)
    c = wts["b_ga"].shape[0]
    u = _matmul_f32w(xb, [(wts["w_in"], 0), (wts["w_in"], c)],
                     [_row(wts["b_ga"]), _row(wts["b_gb"])], [F32], _ep_glu, n=c, tn=256,
                     transposed=True)[0]
    gates = _matmul_f32w(xb, [(wts["w_gate"], 0)], [_row(wts["b_gate"])], [F32],
                         _ep_sigmoid_bias, n=wts["b_gate"].shape[0], transposed=True)[0]
    qn, ckv, ckvb, kr, krb = _latents(xb, wts["w_lat"], wts["b_lat"], wts["q_a_g"],
                                      wts["kv_a_g"], tabs)
    conv_act = _conv(u, u if zero_period else hist, wts["w_dw"], wts["b_dw"],
                     wts["conv_ln_g"], wts["conv_ln_b"], zero_period=zero_period)
    n_q = N_HEADS * HEAD_SLOT
    tn_q = _tile(n_q, 1024)
    q = _matmul_f32w(qn, [(wts["w_q"], 0)], [("rows", t) for t in tabs], [BF16],
                     functools.partial(_ep_q, scale=scale), n=n_q, tn=tn_q,
                     w_block=tn_q // HEAD_SLOT * QK_DIM, prep=_pad_q_weight)[0]
    return dict(u=u, gates=gates, ckv=ckv, ckvb=ckvb, kr=kr, krb=krb, conv_act=conv_act, q=q)


def kernel(x_prompt, x_sample, cache_ckv, cache_krope, state_conv, w_in, b_in, w_dw, b_dw,
           conv_ln_g, conv_ln_b, w_conv_pw, q_a_g, w_q_b, kv_a_g, w_kv_b, w_attn_o, w_out,
           ln1_g, ln1_b, w_up, w_down, ln2_g, ln2_b):
    depth = w_in.shape[0]
    bp, sp, d = x_prompt.shape
    bs, ts, _ = x_sample.shape
    past = cache_ckv.shape[2]
    conv_dim = w_dw.shape[2]
    q_rank, kv_rank = q_a_g.shape[1], kv_a_g.shape[1]
    alpha = (2 * depth) ** 0.25
    scale = QK_DIM ** -0.5 * math.log2(math.e)
    hist_len = CONV_WIDTH - 1
    assert ts == CONV_ROWS and sp % CONV_ROWS == 0

    tabs_p = [jnp.tile(t, (bp, 1)) for t in _rope_tables(jnp.arange(sp))]
    tabs_s = [jnp.tile(t, (bs, 1)) for t in _rope_tables(past + jnp.arange(ts))]

    hp = x_prompt.reshape(bp * sp, d)
    hs = x_sample.reshape(bs * ts, d)
    outs = [[] for _ in range(6)]
    for l in range(depth):
        wts = _prep_weights(l, w_in, b_in, w_q_b, w_kv_b, d, conv_dim, q_rank, kv_rank)
        wts.update(
            w_dw=w_dw[l], b_dw=b_dw[l], conv_ln_g=conv_ln_g[l], conv_ln_b=conv_ln_b[l],
            q_a_g=q_a_g[l], kv_a_g=kv_a_g[l], w_pw=w_conv_pw[l],
            w_o=w_attn_o[l].astype(BF16), w_out=w_out[l],
            ln1_g=ln1_g[l], ln1_b=ln1_b[l], w_up=w_up[l],
            w_down=w_down[l].astype(BF16), ln2_g=ln2_g[l], ln2_b=ln2_b[l])

        p = _in_stage(hp, tabs_p, wts, None, zero_period=sp // CONV_ROWS, scale=scale)
        kv = _matmul_f32w(p["ckvb"], [(wts["w_kv"], 0)], [], [BF16], _ep_cast,
                          n=wts["w_kv"].shape[1], tn=2048)[0]
        p["attn"] = _attn_prompt(p["q"], kv, p["krb"], batch=bp, seq=sp)
        hp_new = _token_mixer(hp, wts, p, alpha=alpha)
        outs[0].append(p["ckv"].reshape(bp, sp, kv_rank))
        outs[1].append(p["kr"].reshape(bp, sp, ROPE_DIM))
        outs[2].append(p["u"].reshape(bp, sp, conv_dim)[:, sp - hist_len:])

        hist = jnp.pad(state_conv[l], ((0, 0), (HIST_PAD, 0), (0, 0))).reshape(bs * HIST_ROWS, conv_dim)
        s = _in_stage(hs, tabs_s, wts, hist, zero_period=0, scale=scale)
        qa, qr = _absorb_q(s["q"], wts["w_kv"], batch=bs, t=ts)
        cache_k = jnp.pad(cache_krope[l], ((0, 0), (0, 0), (0, ROPE_SLOT - ROPE_DIM))).astype(BF16)
        o_lat = _attn_sample(qa, qr, cache_ckv[l].astype(BF16), cache_k,
                             s["ckvb"].reshape(bs, ts, kv_rank), s["krb"].reshape(bs, ts, ROPE_SLOT))
        s["attn"] = _unabsorb(o_lat, wts["w_kv"], t=ts)
        hs_new = _token_mixer(hs, wts, s, alpha=alpha)
        outs[3].append(s["ckv"].reshape(bs, ts, kv_rank))
        outs[4].append(s["kr"].reshape(bs, ts, ROPE_DIM))
        outs[5].append(s["u"].reshape(bs, ts, conv_dim)[:, ts - hist_len:])
        hp, hs = hp_new, hs_new

    return (hp.reshape(bp, sp, d), hs.reshape(bs, ts, d), jnp.stack(outs[0]), jnp.stack(outs[1]),
            jnp.stack(outs[2]), jnp.stack(outs[3]), jnp.stack(outs[4]), jnp.stack(outs[5]))
```

```python
import functools
import math

import jax
import jax.numpy as jnp
from jax import lax
from jax.experimental import pallas as pl
from jax.experimental.pallas import tpu as pltpu

F32 = jnp.float32
BF16 = jnp.bfloat16

CHUNK = 64
CONV_WIDTH = 31
N_HEADS = 64
NOPE_DIM = 128
ROPE_DIM = 64
V_DIM = 128
QK_DIM = NOPE_DIM + ROPE_DIM
ROPE_THETA = 10000.0
LN_EPS = 1e-5
RMS_EPS = 1e-6
NEG_INF = -1e30

LANES = 128
SUBLANES = 8
VMEM_LIMIT_BYTES = 60 * 1024 * 1024

ROPE_SLOT = LANES
HEAD_SLOT = NOPE_DIM + ROPE_SLOT
HIST_ROWS = 32
HIST_PAD = HIST_ROWS - (CONV_WIDTH - 1)
CONV_ROWS = 64
ONES_ROWS = 2 * SUBLANES


def _tile(dim, pref):
    t = min(dim, pref)
    assert dim % t == 0, (dim, pref)
    return t


def _params(sem):
    return pltpu.CompilerParams(dimension_semantics=sem, vmem_limit_bytes=VMEM_LIMIT_BYTES)


def _mm_body(*refs, nw, ne, no, nk, epilogue):
    x_ref = refs[0]
    w_refs = refs[1:1 + nw]
    ex = refs[1 + nw:1 + nw + ne]
    outs = refs[1 + nw + ne:1 + nw + ne + no]
    accs = refs[1 + nw + ne + no:]
    parts = [jnp.dot(x_ref[...], w[...], preferred_element_type=F32) for w in w_refs]
    if nk == 1:
        epilogue(parts, ex, outs)
        return
    k = pl.program_id(2)

    @pl.when(k == 0)
    def _():
        for a, p in zip(accs, parts):
            a[...] = p

    @pl.when(jnp.logical_and(k > 0, k < nk - 1))
    def _():
        for a, p in zip(accs, parts):
            a[...] += p

    @pl.when(k == nk - 1)
    def _():
        epilogue([a[...] + p for a, p in zip(accs, parts)], ex, outs)


def _extra_spec(kind, arr, tm, tn, n, ij):
    if kind == "row":
        return pl.BlockSpec((1, tn), lambda *g: (0, ij(*g)[1]))
    if kind == "rows":
        return pl.BlockSpec((tm, arr.shape[1]), lambda *g: (ij(*g)[0], 0))
    off = kind * (n // tn)
    return pl.BlockSpec((tm, tn), lambda *g: (ij(*g)[0], ij(*g)[1] + off))


def _matmul(x, ws, extras, outs, epilogue, *, tm=1024, tn=1024, tk=4096):
    m, kdim = x.shape
    n = ws[0].shape[1]
    tm, tn, tk = _tile(m, tm), _tile(n, tn), _tile(kdim, tk)
    nk = kdim // tk
    in_specs = [pl.BlockSpec((tm, tk), lambda i, j, k: (i, k))]
    in_specs += [pl.BlockSpec((tk, tn), lambda i, j, k: (k, j)) for _ in ws]
    in_specs += [_extra_spec(kind, arr, tm, tn, n, lambda i, j, k: (i, j)) for kind, arr in extras]
    out_specs = [pl.BlockSpec((tm, tn), lambda i, j, k: (i, j)) for _ in outs]
    out_shape = [jax.ShapeDtypeStruct((m, n), dt) for dt in outs]
    scratch = [pltpu.VMEM((tm, tn), F32) for _ in ws] if nk > 1 else []
    body = functools.partial(_mm_body, nw=len(ws), ne=len(extras), no=len(outs), nk=nk,
                             epilogue=epilogue)
    return pl.pallas_call(
        body,
        grid=(m // tm, n // tn, nk),
        in_specs=in_specs,
        out_specs=out_specs,
        out_shape=out_shape,
        scratch_shapes=scratch,
        compiler_params=_params(("parallel", "parallel", "arbitrary")),
        name="mm" + getattr(epilogue, "func", epilogue).__name__,
    )(x, *ws, *[arr for _, arr in extras])


def _row(vec):
    return ("row", vec.reshape(1, -1))


def _cast_weight(w_ref, wb_ref):
    wb_ref[...] = w_ref[...].astype(BF16)


def _pad_q_weight(w_ref, wb_ref):
    for h in range(wb_ref.shape[1] // HEAD_SLOT):
        c0 = h * HEAD_SLOT
        wb_ref[:, c0:c0 + QK_DIM] = w_ref[:, h * QK_DIM:(h + 1) * QK_DIM].astype(BF16)
        wb_ref[:, c0 + QK_DIM:c0 + HEAD_SLOT] = jnp.zeros(
            (wb_ref.shape[0], HEAD_SLOT - QK_DIM), BF16)


def _mmw_body(*refs, nw, ne, no, epilogue, prep, transposed):
    x_ref = refs[0]
    w_refs = refs[1:1 + nw]
    ex = refs[1 + nw:1 + nw + ne]
    outs = refs[1 + nw + ne:1 + nw + ne + no]
    wb_refs = refs[1 + nw + ne + no:]

    @pl.when(pl.program_id(1) == 0)
    def _():
        for w, wb in zip(w_refs, wb_refs):
            prep(w, wb)

    dims = (((1,), (1 if transposed else 0,)), ((), ()))
    parts = [lax.dot_general(x_ref[...], wb[...], dims, preferred_element_type=F32)
             for wb in wb_refs]
    epilogue(parts, ex, outs)


def _matmul_f32w(x, ws, extras, outs, epilogue, *, n, tm=1024, tn=512, w_block=None,
                 prep=_cast_weight, transposed=False):
    m, kdim = x.shape
    tm, tn = _tile(m, tm), _tile(n, tn)
    w_block = w_block or tn
    ij = lambda j, i: (i, j)
    in_specs = [pl.BlockSpec((tm, kdim), lambda j, i: (i, 0))]
    for _, col0 in ws:
        assert col0 % w_block == 0
        in_specs.append(pl.BlockSpec(
            (w_block, kdim) if transposed else (kdim, w_block),
            functools.partial(_w_index, col0 // w_block, transposed)))
    in_specs += [_extra_spec(kind, arr, tm, tn, n, ij) for kind, arr in extras]
    body = functools.partial(_mmw_body, nw=len(ws), ne=len(extras), no=len(outs),
                             epilogue=epilogue, prep=prep, transposed=transposed)
    return pl.pallas_call(
        body,
        grid=(n // tn, m // tm),
        in_specs=in_specs,
        out_specs=[pl.BlockSpec((tm, tn), lambda j, i: (i, j)) for _ in outs],
        out_shape=[jax.ShapeDtypeStruct((m, n), dt) for dt in outs],
        scratch_shapes=[pltpu.VMEM((tn, kdim) if transposed else (kdim, tn), BF16) for _ in ws],
        compiler_params=_params(("parallel", "arbitrary")),
        name="mmw" + getattr(epilogue, "func", epilogue).__name__,
    )(x, *[w for w, _ in ws], *[arr for _, arr in extras])


def _w_index(off, transposed, j, i):
    return (j + off, 0) if transposed else (0, j + off)


def _ep_glu(parts, ex, outs):
    a = parts[0] + ex[0][...]
    b = parts[1] + ex[1][...]
    outs[0][...] = a * jax.nn.sigmoid(b)


def _ep_sigmoid_bias(parts, ex, outs):
    outs[0][...] = jax.nn.sigmoid(parts[0] + ex[0][...])


def _rope_slot(a, cos, sin_lo, sin_hi):
    half = ROPE_DIM // 2
    return (a * cos + pltpu.roll(a, ROPE_SLOT - half, axis=1) * sin_lo
            + pltpu.roll(a, half, axis=1) * sin_hi)


def _ep_q(parts, ex, outs, *, scale):
    q = parts[0]
    cos, sin_lo, sin_hi = ex[0][...], ex[1][...], ex[2][...]
    for s in range(q.shape[1] // HEAD_SLOT):
        c0 = s * HEAD_SLOT
        outs[0][:, c0:c0 + NOPE_DIM] = (q[:, c0:c0 + NOPE_DIM] * scale).astype(BF16)
        r = _rope_slot(q[:, c0 + NOPE_DIM:c0 + HEAD_SLOT], cos, sin_lo, sin_hi)
        outs[0][:, c0 + NOPE_DIM:c0 + HEAD_SLOT] = (r * scale).astype(BF16)


def _ep_cast(parts, ex, outs):
    outs[0][...] = parts[0].astype(outs[0].dtype)


def _ep_gate(parts, ex, outs):
    outs[0][...] = ex[0][...] * parts[0]


def _ep_merge(parts, ex, outs):
    outs[0][...] = (ex[0][...] * parts[0] + ex[1][...]).astype(BF16)


def _ep_residual(parts, ex, outs, *, alpha):
    outs[0][...] = alpha * ex[0][...] + parts[0]


def _ep_relu2(parts, ex, outs):
    r = jnp.maximum(parts[0], 0.0)
    outs[0][...] = (r * r).astype(BF16)


def _rms(x, g):
    return x * lax.rsqrt(jnp.mean(x * x, axis=-1, keepdims=True) + RMS_EPS) * g


def _latent_body(x_ref, w_ref, b_ref, qg_ref, kvg_ref, cos_ref, slo_ref, shi_ref,
                 qn_ref, ckv_ref, ckvb_ref, kr_ref, krb_ref, *, q_rank, kv_rank):
    acc = lax.dot_general(x_ref[...], w_ref[...], (((1,), (1,)), ((), ())),
                          preferred_element_type=F32) + b_ref[...]
    qn_ref[...] = _rms(acc[:, :q_rank], qg_ref[...]).astype(BF16)
    ckv = _rms(acc[:, q_rank:q_rank + kv_rank], kvg_ref[...])
    ckv_ref[...] = ckv
    ckvb_ref[...] = ckv.astype(BF16)
    kr = _rope_slot(acc[:, q_rank + kv_rank:], cos_ref[...], slo_ref[...], shi_ref[...])
    kr_ref[...] = kr[:, :ROPE_DIM]
    krb_ref[...] = kr.astype(BF16)


def _latents(xb, w_lat, b_lat, q_g, kv_g, tabs, *, tm=512):
    m, d = xb.shape
    q_rank, kv_rank = q_g.shape[0], kv_g.shape[0]
    n = w_lat.shape[0]
    tm = _tile(m, tm)
    row = lambda i: (i, 0)
    fix = lambda i: (0, 0)
    body = functools.partial(_latent_body, q_rank=q_rank, kv_rank=kv_rank)
    return pl.pallas_call(
        body,
        grid=(m // tm,),
        in_specs=[pl.BlockSpec((tm, d), row), pl.BlockSpec((n, d), fix),
                  pl.BlockSpec((1, n), fix), pl.BlockSpec((1, q_rank), fix),
                  pl.BlockSpec((1, kv_rank), fix)]
                 + [pl.BlockSpec((tm, ROPE_SLOT), row)] * 3,
        out_specs=[pl.BlockSpec((tm, q_rank), row), pl.BlockSpec((tm, kv_rank), row),
                   pl.BlockSpec((tm, kv_rank), row), pl.BlockSpec((tm, ROPE_DIM), row),
                   pl.BlockSpec((tm, ROPE_SLOT), row)],
        out_shape=[jax.ShapeDtypeStruct((m, q_rank), BF16),
                   jax.ShapeDtypeStruct((m, kv_rank), F32),
                   jax.ShapeDtypeStruct((m, kv_rank), BF16),
                   jax.ShapeDtypeStruct((m, ROPE_DIM), F32),
                   jax.ShapeDtypeStruct((m, ROPE_SLOT), BF16)],
        compiler_params=_params(("parallel",)),
        name="latents",
    )(xb, w_lat, b_lat.reshape(1, n), q_g.reshape(1, -1), kv_g.reshape(1, -1), *tabs)


def _conv_body(hist_ref, u_ref, w_ref, bdw_ref, g_ref, b_ref, o_ref, win_ref, y_ref,
               *, zero_period, lane_chunk):
    c = u_ref.shape[1]
    hist = hist_ref[...]
    if zero_period:
        first = (pl.program_id(0) % zero_period) == 0
        hist = jnp.where(first, 0.0, hist)
    win_ref[0:HIST_ROWS, :] = hist
    win_ref[HIST_ROWS:, :] = u_ref[...]
    for c0 in range(0, c, lane_chunk):
        lanes = slice(c0, c0 + lane_chunk)
        acc = bdw_ref[:, lanes]
        for r in range(SUBLANES):
            rows = CONV_ROWS if r == 0 else CONV_ROWS + SUBLANES
            z = None
            for a in range((HIST_PAD + CONV_WIDTH - 1 - r) // SUBLANES + 1):
                k = a * SUBLANES + r - HIST_PAD
                if k < 0:
                    continue
                term = win_ref[a * SUBLANES:a * SUBLANES + rows, lanes] * w_ref[k:k + 1, lanes]
                z = term if z is None else z + term
            acc = acc + z[r:r + CONV_ROWS]
        y_ref[:, lanes] = acc
    y = y_ref[...]
    mu = jnp.mean(y, axis=-1, keepdims=True)
    yc = y - mu
    var = jnp.mean(yc * yc, axis=-1, keepdims=True)
    z = yc * lax.rsqrt(var + LN_EPS) * g_ref[...] + b_ref[...]
    o_ref[...] = (z * jax.nn.sigmoid(z)).astype(BF16)


def _conv(u, hist, w_dw, b_dw, g, b, *, zero_period):
    m, c = u.shape
    w_pad = jnp.pad(w_dw, ((0, HIST_ROWS - CONV_WIDTH), (0, 0)))
    per = CONV_ROWS // HIST_ROWS
    if zero_period:
        hist_map = lambda i: (jnp.maximum(i * per - 1, 0), 0)
    else:
        hist_map = lambda i: (i, 0)
    fix = lambda i: (0, 0)
    body = functools.partial(_conv_body, zero_period=zero_period, lane_chunk=min(c, 2 * LANES))
    return pl.pallas_call(
        body,
        grid=(m // CONV_ROWS,),
        in_specs=[pl.BlockSpec((HIST_ROWS, c), hist_map),
                  pl.BlockSpec((CONV_ROWS, c), lambda i: (i, 0)),
                  pl.BlockSpec((HIST_ROWS, c), fix),
                  pl.BlockSpec((1, c), fix), pl.BlockSpec((1, c), fix), pl.BlockSpec((1, c), fix)],
        out_specs=pl.BlockSpec((CONV_ROWS, c), lambda i: (i, 0)),
        out_shape=jax.ShapeDtypeStruct((m, c), BF16),
        scratch_shapes=[pltpu.VMEM((HIST_ROWS + CONV_ROWS, c), F32),
                        pltpu.VMEM((CONV_ROWS, c), F32)],
        compiler_params=_params(("parallel",)),
        name="conv",
    )(hist, u, w_pad, b_dw.reshape(1, c), g.reshape(1, c), b.reshape(1, c))


def _ln_body(x_ref, g_ref, b_ref, *o_refs):
    x = x_ref[...]
    mu = jnp.mean(x, axis=-1, keepdims=True)
    xc = x - mu
    var = jnp.mean(xc * xc, axis=-1, keepdims=True)
    y = xc * lax.rsqrt(var + LN_EPS) * g_ref[...] + b_ref[...]
    for o in o_refs:
        o[...] = y.astype(o.dtype)


def _layer_norm(x, g, b, dtypes, *, tm=256):
    m, d = x.shape
    tm = _tile(m, tm)
    row = lambda i: (i, 0)
    fix = lambda i: (0, 0)
    return pl.pallas_call(
        _ln_body,
        grid=(m // tm,),
        in_specs=[pl.BlockSpec((tm, d), row), pl.BlockSpec((1, d), fix), pl.BlockSpec((1, d), fix)],
        out_specs=[pl.BlockSpec((tm, d), row) for _ in dtypes],
        out_shape=[jax.ShapeDtypeStruct((m, d), dt) for dt in dtypes],
        compiler_params=_params(("parallel",)),
        name="layer_norm",
    )(x, g.reshape(1, d), b.reshape(1, d))


def _row_reduce(x, combine, reduce):
    part = x[:, :LANES]
    for c in range(LANES, x.shape[1], LANES):
        part = combine(part, x[:, c:c + LANES])
    return reduce(part, axis=1, keepdims=True)


def _attn_prompt_body(q_ref, kn_ref, kr_ref, v_ref, o_ref, kf_ref, vt_ref, qt_ref, s0_ref, s1_ref,
                      p0_ref, p1_ref, acc0_ref, acc1_ref, *, tq):
    seq = q_ref.shape[0]
    kf_ref[:, :NOPE_DIM] = kn_ref[...]
    kf_ref[:, NOPE_DIM:] = kr_ref[...]
    for t in range(seq // tq):
        vt_ref[t, :V_DIM, :] = v_ref[t * tq:(t + 1) * tq, :].T
        vt_ref[t, V_DIM:, :] = jnp.ones((ONES_ROWS, tq), BF16)
        qt_ref[t] = q_ref[t * tq:(t + 1) * tq, :].T
    key_chunk = lax.broadcasted_iota(jnp.int32, (tq, tq), 0) // CHUNK
    q_chunk = lax.broadcasted_iota(jnp.int32, (tq, tq), 1) // CHUNK
    visible = key_chunk <= q_chunk

    def scores(qi, kj):
        return jnp.dot(kf_ref[kj * tq:(kj + 1) * tq, :], qt_ref[qi], preferred_element_type=F32)

    def over_keys(x, combine, reduce):
        part = x[0:tq // SUBLANES]
        for g in range(1, SUBLANES):
            part = combine(part, x[g * tq // SUBLANES:(g + 1) * tq // SUBLANES])
        return reduce(part, axis=0, keepdims=True)

    s_refs, p_refs, acc_refs = (s0_ref, s1_ref), (p0_ref, p1_ref), (acc0_ref, acc1_ref)
    step_no = 0
    for qi in range(seq // tq):
        acc_ref = acc_refs[qi % 2]
        m = jnp.full((1, tq), NEG_INF, F32)
        acc = a_prev = None
        s_refs[step_no % 2][...] = scores(qi, 0)
        for j in range(qi + 1):
            cur, nxt = step_no % 2, (step_no + 1) % 2
            last = j == qi
            if not last:
                s_refs[nxt][...] = scores(qi, j + 1)
            if j > 0:
                pv_prev = jnp.dot(vt_ref[j - 1], p_refs[nxt][...], preferred_element_type=F32)
                acc = pv_prev if j == 1 else acc_ref[...] * a_prev + pv_prev
            s = s_refs[cur][...]
            if last:
                s = jnp.where(visible, s, NEG_INF)
            m_new = jnp.maximum(m, over_keys(s, jnp.maximum, jnp.max))
            alpha = jnp.exp2(m - m_new)
            p = jnp.exp2(s - m_new)
            m = m_new
            p_refs[cur][...] = p.astype(BF16)
            if last:
                pv_last = jnp.dot(vt_ref[j], p_refs[cur][...], preferred_element_type=F32)
                acc = pv_last if acc is None else acc * alpha + pv_last
                out_t = acc[:V_DIM] / acc[V_DIM:V_DIM + 1]
                o_ref[qi * tq:(qi + 1) * tq, :] = out_t.T.astype(o_ref.dtype)
            else:
                if acc is not None:
                    acc_ref[...] = acc
                a_prev = alpha
            step_no += 1


def _attn_prompt(q, kv, kr, *, batch, seq, tq=512):
    tq = _tile(seq, tq)
    assert tq % CHUNK == 0
    per_head = (NOPE_DIM + V_DIM) // LANES
    body = functools.partial(_attn_prompt_body, tq=tq)
    return pl.pallas_call(
        body,
        grid=(batch, N_HEADS),
        in_specs=[pl.BlockSpec((seq, HEAD_SLOT), lambda b, h: (b, h)),
                  pl.BlockSpec((seq, NOPE_DIM), lambda b, h: (b, per_head * h)),
                  pl.BlockSpec((seq, ROPE_SLOT), lambda b, h: (b, 0)),
                  pl.BlockSpec((seq, V_DIM), lambda b, h: (b, per_head * h + 1))],
        out_specs=pl.BlockSpec((seq, V_DIM), lambda b, h: (b, h)),
        out_shape=jax.ShapeDtypeStruct((batch * seq, N_HEADS * V_DIM), BF16),
        scratch_shapes=[pltpu.VMEM((seq, HEAD_SLOT), BF16),
                        pltpu.VMEM((seq // tq, V_DIM + ONES_ROWS, tq), BF16),
                        pltpu.VMEM((seq // tq, HEAD_SLOT, tq), BF16),
                        pltpu.VMEM((tq, tq), F32), pltpu.VMEM((tq, tq), F32),
                        pltpu.VMEM((tq, tq), BF16), pltpu.VMEM((tq, tq), BF16),
                        pltpu.VMEM((V_DIM + ONES_ROWS, tq), F32),
                        pltpu.VMEM((V_DIM + ONES_ROWS, tq), F32)],
        compiler_params=_params(("parallel", "parallel")),
        name="attn_prompt",
    )(q, kv, kr, kv)


def _absorb_q_body(qn_ref, qr_ref, wuk_ref, qa_ref, qro_ref, *, batch, t):
    qa = lax.dot_general(qn_ref[...], wuk_ref[...].astype(BF16), (((1,), (1,)), ((), ())),
                         preferred_element_type=F32).astype(BF16)
    for b in range(batch):
        qa_ref[b] = qa[b * t:(b + 1) * t]
        qro_ref[b] = qr_ref[b * t:(b + 1) * t, :]


def _absorb_q(q, w_kv, *, batch, t):
    m = batch * t
    kv_rank = w_kv.shape[0]
    per_head = (NOPE_DIM + V_DIM) // LANES
    q_per_head = HEAD_SLOT // LANES
    body = functools.partial(_absorb_q_body, batch=batch, t=t)
    return pl.pallas_call(
        body,
        grid=(N_HEADS,),
        in_specs=[pl.BlockSpec((m, NOPE_DIM), lambda h: (0, q_per_head * h)),
                  pl.BlockSpec((m, ROPE_SLOT), lambda h: (0, q_per_head * h + 1)),
                  pl.BlockSpec((kv_rank, NOPE_DIM), lambda h: (0, per_head * h))],
        out_specs=[pl.BlockSpec((batch, t, kv_rank), lambda h: (0, h, 0)),
                   pl.BlockSpec((batch, t, ROPE_SLOT), lambda h: (0, h, 0))],
        out_shape=[jax.ShapeDtypeStruct((batch, N_HEADS * t, kv_rank), BF16),
                   jax.ShapeDtypeStruct((batch, N_HEADS * t, ROPE_SLOT), BF16)],
        compiler_params=_params(("parallel",)),
        name="absorb_q",
    )(q, q, w_kv)


def _attn_sample_body(qa_ref, qr_ref, cc_ref, ck_ref, nc_ref, nk_ref, o_ref):
    qa, qr = qa_ref[0], qr_ref[0]
    cc, ck, nc, nk = cc_ref[0], ck_ref[0], nc_ref[0], nk_ref[0]
    dims = (((1,), (1,)), ((), ()))
    s_old = (lax.dot_general(qa, cc, dims, preferred_element_type=F32)
             + lax.dot_general(qr, ck, dims, preferred_element_type=F32))
    s_new = (lax.dot_general(qa, nc, dims, preferred_element_type=F32)
             + lax.dot_general(qr, nk, dims, preferred_element_type=F32))
    m = jnp.maximum(_row_reduce(s_old, jnp.maximum, jnp.max),
                    jnp.max(s_new, axis=1, keepdims=True))
    p_old = jnp.exp2(s_old - m)
    p_new = jnp.exp2(s_new - m)
    l = _row_reduce(p_old, jnp.add, jnp.sum) + jnp.sum(p_new, axis=1, keepdims=True)
    o = (jnp.dot(p_old.astype(BF16), cc, preferred_element_type=F32)
         + jnp.dot(p_new.astype(BF16), nc, preferred_element_type=F32))
    o_ref[0] = (o / l).astype(BF16)


def _attn_sample(qa, qr, cache_c, cache_k, new_c, new_k, *, tr=512):
    batch, rows, kv_rank = qa.shape
    past, t = cache_c.shape[1], new_c.shape[1]
    tr = _tile(rows, tr)
    return pl.pallas_call(
        _attn_sample_body,
        grid=(batch, rows // tr),
        in_specs=[pl.BlockSpec((1, tr, kv_rank), lambda b, r: (b, r, 0)),
                  pl.BlockSpec((1, tr, ROPE_SLOT), lambda b, r: (b, r, 0)),
                  pl.BlockSpec((1, past, kv_rank), lambda b, r: (b, 0, 0)),
                  pl.BlockSpec((1, past, ROPE_SLOT), lambda b, r: (b, 0, 0)),
                  pl.BlockSpec((1, t, kv_rank), lambda b, r: (b, 0, 0)),
                  pl.BlockSpec((1, t, ROPE_SLOT), lambda b, r: (b, 0, 0))],
        out_specs=pl.BlockSpec((1, tr, kv_rank), lambda b, r: (b, r, 0)),
        out_shape=jax.ShapeDtypeStruct((batch, rows, kv_rank), BF16),
        compiler_params=_params(("parallel", "parallel")),
        name="attn_sample",
    )(qa, qr, cache_c, cache_k, new_c, new_k)


def _unabsorb_body(o_ref, wuv_ref, out_ref):
    b, t, r = o_ref.shape
    out_ref[...] = jnp.dot(o_ref[...].reshape(b * t, r), wuv_ref[...].astype(BF16),
                           preferred_element_type=F32).astype(BF16)


def _unabsorb(o_lat, w_kv, *, t):
    batch, _, kv_rank = o_lat.shape
    per_head = (NOPE_DIM + V_DIM) // LANES
    return pl.pallas_call(
        _unabsorb_body,
        grid=(N_HEADS,),
        in_specs=[pl.BlockSpec((batch, t, kv_rank), lambda h: (0, h, 0)),
                  pl.BlockSpec((kv_rank, V_DIM), lambda h: (0, per_head * h + 1))],
        out_specs=pl.BlockSpec((batch * t, V_DIM), lambda h: (0, h)),
        out_shape=jax.ShapeDtypeStruct((batch * t, N_HEADS * V_DIM), BF16),
        compiler_params=_params(("parallel",)),
        name="unabsorb",
    )(o_lat, w_kv)


def _rope_tables(pos):
    half = ROPE_DIM // 2
    inv = ROPE_THETA ** (-jnp.arange(half, dtype=F32) / half)
    ang = pos.astype(F32)[:, None] * inv[None, :]
    cos, sin = jnp.cos(ang), jnp.sin(ang)
    zero = jnp.zeros_like(cos)
    cos_t = jnp.concatenate([cos, cos, zero, zero], axis=1)
    sin_lo = jnp.concatenate([-sin, zero, zero, zero], axis=1)
    sin_hi = jnp.concatenate([zero, sin, zero, zero], axis=1)
    return cos_t, sin_lo, sin_hi


def _prep_weights(l, w_in, b_in, w_q_b, w_kv_b, d, conv_dim, q_rank, kv_rank):
    w, b = jnp.swapaxes(w_in[l], 0, 1), b_in[l]
    o_q = 2 * conv_dim
    o_kv = o_q + q_rank
    o_kr = o_kv + kv_rank
    o_g = o_kr + ROPE_DIM
    pad = ROPE_SLOT - ROPE_DIM
    w_lat = jnp.concatenate([w[o_q:o_g], jnp.zeros((pad, d), w.dtype)], axis=0).astype(BF16)
    b_lat = jnp.concatenate([b[o_q:o_g], jnp.zeros((pad,), b.dtype)])
    return dict(
        w_in=w, b_ga=b[:conv_dim], b_gb=b[conv_dim:o_q], w_lat=w_lat, b_lat=b_lat,
        w_gate=w[o_g:], b_gate=b[o_g:], w_q=w_q_b[l], w_kv=w_kv_b[l])


def _token_mixer(x, wts, p, *, alpha):
    residual = functools.partial(_ep_residual, alpha=alpha)
    d = x.shape[1]
    gconv = _matmul_f32w(p["conv_act"], [(wts["w_pw"], 0)], [(0, p["gates"])], [F32], _ep_gate,
                         n=d)[0]
    merged = _matmul(p["attn"], [wts["w_o"]], [(1, p["gates"]), (0, gconv)], [BF16], _ep_merge,
                     tm=1024, tn=256, tk=8192)[0]
    pre1 = _matmul_f32w(merged, [(wts["w_out"], 0)], [(0, x)], [F32], residual, n=d)[0]
    h, hb = _layer_norm(pre1, wts["ln1_g"], wts["ln1_b"], [F32, BF16])
    act = _matmul_f32w(hb, [(wts["w_up"], 0)], [], [BF16], _ep_relu2,
                       n=wts["w_up"].shape[1])[0]
    pre2 = _matmul(act, [wts["w_down"]], [(0, h)], [F32], residual, tk=2048)[0]
    return _layer_norm(pre2, wts["ln2_g"], wts["ln2_b"], [F32])[0]


def _in_stage(x, tabs, wts, hist, *, zero_period, scale):
    xb = x.astype(BF16)
    c = wts["b_ga"].shape[0]
    u = _matmul_f32w(xb, [(wts["w_in"], 0), (wts["w_in"], c)],
                     [_row(wts["b_ga"]), _row(wts["b_gb"])], [F32], _ep_glu, n=c, tn=256,
                     transposed=True)[0]
    gates = _matmul_f32w(xb, [(wts["w_gate"], 0)], [_row(wts["b_gate"])], [F32],
                         _ep_sigmoid_bias, n=wts["b_gate"].shape[0], transposed=True)[0]
    qn, ckv, ckvb, kr, krb = _latents(xb, wts["w_lat"], wts["b_lat"], wts["q_a_g"],
                                      wts["kv_a_g"], tabs)
    conv_act = _conv(u, u if zero_period else hist, wts["w_dw"], wts["b_dw"],
                     wts["conv_ln_g"], wts["conv_ln_b"], zero_period=zero_period)
    n_q = N_HEADS * HEAD_SLOT
    tn_q = _tile(n_q, 1024)
    q = _matmul_f32w(qn, [(wts["w_q"], 0)], [("rows", t) for t in tabs], [BF16],
                     functools.partial(_ep_q, scale=scale), n=n_q, tn=tn_q,
                     w_block=tn_q // HEAD_SLOT * QK_DIM, prep=_pad_q_weight)[0]
    return dict(u=u, gates=gates, ckv=ckv, ckvb=ckvb, kr=kr, krb=krb, conv_act=conv_act, q=q)


def kernel(x_prompt, x_sample, cache_ckv, cache_krope, state_conv, w_in, b_in, w_dw, b_dw,
           conv_ln_g, conv_ln_b, w_conv_pw, q_a_g, w_q_b, kv_a_g, w_kv_b, w_attn_o, w_out,
           ln1_g, ln1_b, w_up, w_down, ln2_g, ln2_b):
    depth = w_in.shape[0]
    bp, sp, d = x_prompt.shape
    bs, ts, _ = x_sample.shape
    past = cache_ckv.shape[2]
    conv_dim = w_dw.shape[2]
    q_rank, kv_rank = q_a_g.shape[1], kv_a_g.shape[1]
    alpha = (2 * depth) ** 0.25
    scale = QK_DIM ** -0.5 * math.log2(math.e)
    hist_len = CONV_WIDTH - 1
    assert ts == CONV_ROWS and sp % CONV_ROWS == 0

    tabs_p = [jnp.tile(t, (bp, 1)) for t in _rope_tables(jnp.arange(sp))]
    tabs_s = [jnp.tile(t, (bs, 1)) for t in _rope_tables(past + jnp.arange(ts))]

    hp = x_prompt.reshape(bp * sp, d)
    hs = x_sample.reshape(bs * ts, d)
    outs = [[] for _ in range(6)]
    for l in range(depth):
        wts = _prep_weights(l, w_in, b_in, w_q_b, w_kv_b, d, conv_dim, q_rank, kv_rank)
        wts.update(
            w_dw=w_dw[l], b_dw=b_dw[l], conv_ln_g=conv_ln_g[l], conv_ln_b=conv_ln_b[l],
            q_a_g=q_a_g[l], kv_a_g=kv_a_g[l], w_pw=w_conv_pw[l],
            w_o=w_attn_o[l].astype(BF16), w_out=w_out[l],
            ln1_g=ln1_g[l], ln1_b=ln1_b[l], w_up=w_up[l],
            w_down=w_down[l].astype(BF16), ln2_g=ln2_g[l], ln2_b=ln2_b[l])

        p = _in_stage(hp, tabs_p, wts, None, zero_period=sp // CONV_ROWS, scale=scale)
        kv = _matmul_f32w(p["ckvb"], [(wts["w_kv"], 0)], [], [BF16], _ep_cast,
                          n=wts["w_kv"].shape[1], tn=2048)[0]
        p["attn"] = _attn_prompt(p["q"], kv, p["krb"], batch=bp, seq=sp)
        hp_new = _token_mixer(hp, wts, p, alpha=alpha)
        outs[0].append(p["ckv"].reshape(bp, sp, kv_rank))
        outs[1].append(p["kr"].reshape(bp, sp, ROPE_DIM))
        outs[2].append(p["u"].reshape(bp, sp, conv_dim)[:, sp - hist_len:])

        hist = jnp.pad(state_conv[l], ((0, 0), (HIST_PAD, 0), (0, 0))).reshape(bs * HIST_ROWS, conv_dim)
        s = _in_stage(hs, tabs_s, wts, hist, zero_period=0, scale=scale)
        qa, qr = _absorb_q(s["q"], wts["w_kv"], batch=bs, t=ts)
        cache_k = jnp.pad(cache_krope[l], ((0, 0), (0, 0), (0, ROPE_SLOT - ROPE_DIM))).astype(BF16)
        o_lat = _attn_sample(qa, qr, cache_ckv[l].astype(BF16), cache_k,
                             s["ckvb"].reshape(bs, ts, kv_rank), s["krb"].reshape(bs, ts, ROPE_SLOT))
        s["attn"] = _unabsorb(o_lat, wts["w_kv"], t=ts)
        hs_new = _token_mixer(hs, wts, s, alpha=alpha)
        outs[3].append(s["ckv"].reshape(bs, ts, kv_rank))
        outs[4].append(s["kr"].reshape(bs, ts, ROPE_DIM))
        outs[5].append(s["u"].reshape(bs, ts, conv_dim)[:, ts - hist_len:])
        hp, hs = hp_new, hs_new

    return (hp.reshape(bp, sp, d), hs.reshape(bs, ts, d), jnp.stack(outs[0]), jnp.stack(outs[1]),
            jnp.stack(outs[2]), jnp.stack(outs[3]), jnp.stack(outs[4]), jnp.stack(outs[5]))
```

```python
import functools
import math

import jax
import jax.numpy as jnp
from jax import lax
from jax.experimental import pallas as pl
from jax.experimental.pallas import tpu as pltpu

F32 = jnp.float32
BF16 = jnp.bfloat16

CHUNK = 64
CONV_WIDTH = 31
N_HEADS = 64
NOPE_DIM = 128
ROPE_DIM = 64
V_DIM = 128
QK_DIM = NOPE_DIM + ROPE_DIM
ROPE_THETA = 10000.0
LN_EPS = 1e-5
RMS_EPS = 1e-6
NEG_INF = -1e30

LANES = 128
SUBLANES = 8
VMEM_LIMIT_BYTES = 60 * 1024 * 1024

ROPE_SLOT = LANES
HEAD_SLOT = NOPE_DIM + ROPE_SLOT
HIST_ROWS = 32
HIST_PAD = HIST_ROWS - (CONV_WIDTH - 1)
CONV_ROWS = 64
ONES_ROWS = 2 * SUBLANES


def _tile(dim, pref):
    t = min(dim, pref)
    assert dim % t == 0, (dim, pref)
    return t


def _params(sem):
    return pltpu.CompilerParams(dimension_semantics=sem, vmem_limit_bytes=VMEM_LIMIT_BYTES)


def _mm_body(*refs, nw, ne, no, nk, epilogue):
    x_ref = refs[0]
    w_refs = refs[1:1 + nw]
    ex = refs[1 + nw:1 + nw + ne]
    outs = refs[1 + nw + ne:1 + nw + ne + no]
    accs = refs[1 + nw + ne + no:]
    parts = [jnp.dot(x_ref[...], w[...], preferred_element_type=F32) for w in w_refs]
    if nk == 1:
        epilogue(parts, ex, outs)
        return
    k = pl.program_id(2)

    @pl.when(k == 0)
    def _():
        for a, p in zip(accs, parts):
            a[...] = p

    @pl.when(jnp.logical_and(k > 0, k < nk - 1))
    def _():
        for a, p in zip(accs, parts):
            a[...] += p

    @pl.when(k == nk - 1)
    def _():
        epilogue([a[...] + p for a, p in zip(accs, parts)], ex, outs)


def _extra_spec(kind, arr, tm, tn, n, ij):
    if kind == "row":
        return pl.BlockSpec((1, tn), lambda *g: (0, ij(*g)[1]))
    if kind == "rows":
        return pl.BlockSpec((tm, arr.shape[1]), lambda *g: (ij(*g)[0], 0))
    off = kind * (n // tn)
    return pl.BlockSpec((tm, tn), lambda *g: (ij(*g)[0], ij(*g)[1] + off))


def _matmul(x, ws, extras, outs, epilogue, *, tm=1024, tn=1024, tk=4096):
    m, kdim = x.shape
    n = ws[0].shape[1]
    tm, tn, tk = _tile(m, tm), _tile(n, tn), _tile(kdim, tk)
    nk = kdim // tk
    in_specs = [pl.BlockSpec((tm, tk), lambda i, j, k: (i, k))]
    in_specs += [pl.BlockSpec((tk, tn), lambda i, j, k: (k, j)) for _ in ws]
    in_specs += [_extra_spec(kind, arr, tm, tn, n, lambda i, j, k: (i, j)) for kind, arr in extras]
    out_specs = [pl.BlockSpec((tm, tn), lambda i, j, k: (i, j)) for _ in outs]
    out_shape = [jax.ShapeDtypeStruct((m, n), dt) for dt in outs]
    scratch = [pltpu.VMEM((tm, tn), F32) for _ in ws] if nk > 1 else []
    body = functools.partial(_mm_body, nw=len(ws), ne=len(extras), no=len(outs), nk=nk,
                             epilogue=epilogue)
    return pl.pallas_call(
        body,
        grid=(m // tm, n // tn, nk),
        in_specs=in_specs,
        out_specs=out_specs,
        out_shape=out_shape,
        scratch_shapes=scratch,
        compiler_params=_params(("parallel", "parallel", "arbitrary")),
        name="mm" + getattr(epilogue, "func", epilogue).__name__,
    )(x, *ws, *[arr for _, arr in extras])


def _row(vec):
    return ("row", vec.reshape(1, -1))


def _cast_weight(w_ref, wb_ref):
    w = w_ref[...].astype(BF16)
    wb_ref[...] = w
    return w


def _pad_q_weight(w_ref, wb_ref):
    for h in range(wb_ref.shape[1] // HEAD_SLOT):
        c0 = h * HEAD_SLOT
        wb_ref[:, c0:c0 + QK_DIM] = w_ref[:, h * QK_DIM:(h + 1) * QK_DIM].astype(BF16)
        wb_ref[:, c0 + QK_DIM:c0 + HEAD_SLOT] = jnp.zeros(
            (wb_ref.shape[0], HEAD_SLOT - QK_DIM), BF16)
    return wb_ref[...]


def _mmw_body(*refs, nw, ne, no, epilogue, prep, transposed):
    x_ref = refs[0]
    w_refs = refs[1:1 + nw]
    ex = refs[1 + nw:1 + nw + ne]
    outs = refs[1 + nw + ne:1 + nw + ne + no]
    wb_refs = refs[1 + nw + ne + no:]

    dims = (((1,), (1 if transposed else 0,)), ((), ()))

    def run(weights):
        parts = [lax.dot_general(x_ref[...], w, dims, preferred_element_type=F32)
                 for w in weights]
        epilogue(parts, ex, outs)

    first = pl.program_id(1) == 0

    @pl.when(first)
    def _():
        weights = [prep(w, wb) for w, wb in zip(w_refs, wb_refs)]
        run(weights)

    @pl.when(jnp.logical_not(first))
    def _():
        run([wb[...] for wb in wb_refs])


def _matmul_f32w(x, ws, extras, outs, epilogue, *, n, tm=1024, tn=512, w_block=None,
                 prep=_cast_weight, transposed=False):
    m, kdim = x.shape
    tm, tn = _tile(m, tm), _tile(n, tn)
    w_block = w_block or tn
    ij = lambda j, i: (i, j)
    in_specs = [pl.BlockSpec((tm, kdim), lambda j, i: (i, 0))]
    for _, col0 in ws:
        assert col0 % w_block == 0
        in_specs.append(pl.BlockSpec(
            (w_block, kdim) if transposed else (kdim, w_block),
            functools.partial(_w_index, col0 // w_block, transposed)))
    in_specs += [_extra_spec(kind, arr, tm, tn, n, ij) for kind, arr in extras]
    body = functools.partial(_mmw_body, nw=len(ws), ne=len(extras), no=len(outs),
                             epilogue=epilogue, prep=prep, transposed=transposed)
    return pl.pallas_call(
        body,
        grid=(n // tn, m // tm),
        in_specs=in_specs,
        out_specs=[pl.BlockSpec((tm, tn), lambda j, i: (i, j)) for _ in outs],
        out_shape=[jax.ShapeDtypeStruct((m, n), dt) for dt in outs],
        scratch_shapes=[pltpu.VMEM((tn, kdim) if transposed else (kdim, tn), BF16) for _ in ws],
        compiler_params=_params(("parallel", "arbitrary")),
        name="mmw" + getattr(epilogue, "func", epilogue).__name__,
    )(x, *[w for w, _ in ws], *[arr for _, arr in extras])


def _w_index(off, transposed, j, i):
    return (j + off, 0) if transposed else (0, j + off)


def _ep_glu(parts, ex, outs):
    a = parts[0] + ex[0][...]
    b = parts[1] + ex[1][...]
    outs[0][...] = a * jax.nn.sigmoid(b)


def _ep_sigmoid_bias(parts, ex, outs):
    outs[0][...] = jax.nn.sigmoid(parts[0] + ex[0][...])


def _rope_slot(a, cos, sin_lo, sin_hi):
    half = ROPE_DIM // 2
    return (a * cos + pltpu.roll(a, ROPE_SLOT - half, axis=1) * sin_lo
            + pltpu.roll(a, half, axis=1) * sin_hi)


def _ep_q(parts, ex, outs, *, scale):
    q = parts[0]
    cos, sin_lo, sin_hi = ex[0][...], ex[1][...], ex[2][...]
    for s in range(q.shape[1] // HEAD_SLOT):
        c0 = s * HEAD_SLOT
        outs[0][:, c0:c0 + NOPE_DIM] = (q[:, c0:c0 + NOPE_DIM] * scale).astype(BF16)
        r = _rope_slot(q[:, c0 + NOPE_DIM:c0 + HEAD_SLOT], cos, sin_lo, sin_hi)
        outs[0][:, c0 + NOPE_DIM:c0 + HEAD_SLOT] = (r * scale).astype(BF16)


def _ep_cast(parts, ex, outs):
    outs[0][...] = parts[0].astype(outs[0].dtype)


def _ep_gate(parts, ex, outs):
    outs[0][...] = ex[0][...] * parts[0]


def _ep_merge(parts, ex, outs):
    outs[0][...] = (ex[0][...] * parts[0] + ex[1][...]).astype(BF16)


def _ep_residual(parts, ex, outs, *, alpha):
    outs[0][...] = alpha * ex[0][...] + parts[0]


def _ep_relu2(parts, ex, outs):
    r = jnp.maximum(parts[0], 0.0)
    outs[0][...] = (r * r).astype(BF16)


def _rms(x, g):
    return x * lax.rsqrt(jnp.mean(x * x, axis=-1, keepdims=True) + RMS_EPS) * g


def _latent_body(x_ref, w_ref, b_ref, qg_ref, kvg_ref, cos_ref, slo_ref, shi_ref,
                 qn_ref, ckv_ref, ckvb_ref, kr_ref, krb_ref, *, q_rank, kv_rank):
    acc = lax.dot_general(x_ref[...], w_ref[...], (((1,), (1,)), ((), ())),
                          preferred_element_type=F32) + b_ref[...]
    qn_ref[...] = _rms(acc[:, :q_rank], qg_ref[...]).astype(BF16)
    ckv = _rms(acc[:, q_rank:q_rank + kv_rank], kvg_ref[...])
    ckv_ref[...] = ckv
    ckvb_ref[...] = ckv.astype(BF16)
    kr = _rope_slot(acc[:, q_rank + kv_rank:], cos_ref[...], slo_ref[...], shi_ref[...])
    kr_ref[...] = kr[:, :ROPE_DIM]
    krb_ref[...] = kr.astype(BF16)


def _latents(xb, w_lat, b_lat, q_g, kv_g, tabs, *, tm=512):
    m, d = xb.shape
    q_rank, kv_rank = q_g.shape[0], kv_g.shape[0]
    n = w_lat.shape[0]
    tm = _tile(m, tm)
    row = lambda i: (i, 0)
    fix = lambda i: (0, 0)
    body = functools.partial(_latent_body, q_rank=q_rank, kv_rank=kv_rank)
    return pl.pallas_call(
        body,
        grid=(m // tm,),
        in_specs=[pl.BlockSpec((tm, d), row), pl.BlockSpec((n, d), fix),
                  pl.BlockSpec((1, n), fix), pl.BlockSpec((1, q_rank), fix),
                  pl.BlockSpec((1, kv_rank), fix)]
                 + [pl.BlockSpec((tm, ROPE_SLOT), row)] * 3,
        out_specs=[pl.BlockSpec((tm, q_rank), row), pl.BlockSpec((tm, kv_rank), row),
                   pl.BlockSpec((tm, kv_rank), row), pl.BlockSpec((tm, ROPE_DIM), row),
                   pl.BlockSpec((tm, ROPE_SLOT), row)],
        out_shape=[jax.ShapeDtypeStruct((m, q_rank), BF16),
                   jax.ShapeDtypeStruct((m, kv_rank), F32),
                   jax.ShapeDtypeStruct((m, kv_rank), BF16),
                   jax.ShapeDtypeStruct((m, ROPE_DIM), F32),
                   jax.ShapeDtypeStruct((m, ROPE_SLOT), BF16)],
        compiler_params=_params(("parallel",)),
        name="latents",
    )(xb, w_lat, b_lat.reshape(1, n), q_g.reshape(1, -1), kv_g.reshape(1, -1), *tabs)


def _conv_body(hist_ref, u_ref, w_ref, bdw_ref, g_ref, b_ref, o_ref, win_ref, y_ref,
               *, zero_period, lane_chunk):
    c = u_ref.shape[1]
    hist = hist_ref[...]
    if zero_period:
        first = (pl.program_id(0) % zero_period) == 0
        hist = jnp.where(first, 0.0, hist)
    win_ref[0:HIST_ROWS, :] = hist
    win_ref[HIST_ROWS:, :] = u_ref[...]
    for c0 in range(0, c, lane_chunk):
        lanes = slice(c0, c0 + lane_chunk)
        acc = bdw_ref[:, lanes]
        for r in range(SUBLANES):
            rows = CONV_ROWS if r == 0 else CONV_ROWS + SUBLANES
            z = None
            for a in range((HIST_PAD + CONV_WIDTH - 1 - r) // SUBLANES + 1):
                k = a * SUBLANES + r - HIST_PAD
                if k < 0:
                    continue
                term = win_ref[a * SUBLANES:a * SUBLANES + rows, lanes] * w_ref[k:k + 1, lanes]
                z = term if z is None else z + term
            acc = acc + z[r:r + CONV_ROWS]
        y_ref[:, lanes] = acc
    y = y_ref[...]
    mu = jnp.mean(y, axis=-1, keepdims=True)
    yc = y - mu
    var = jnp.mean(yc * yc, axis=-1, keepdims=True)
    z = yc * lax.rsqrt(var + LN_EPS) * g_ref[...] + b_ref[...]
    o_ref[...] = (z * jax.nn.sigmoid(z)).astype(BF16)


def _conv(u, hist, w_dw, b_dw, g, b, *, zero_period):
    m, c = u.shape
    w_pad = jnp.pad(w_dw, ((0, HIST_ROWS - CONV_WIDTH), (0, 0)))
    per = CONV_ROWS // HIST_ROWS
    if zero_period:
        hist_map = lambda i: (jnp.maximum(i * per - 1, 0), 0)
    else:
        hist_map = lambda i: (i, 0)
    fix = lambda i: (0, 0)
    body = functools.partial(_conv_body, zero_period=zero_period, lane_chunk=min(c, 2 * LANES))
    return pl.pallas_call(
        body,
        grid=(m // CONV_ROWS,),
        in_specs=[pl.BlockSpec((HIST_ROWS, c), hist_map),
                  pl.BlockSpec((CONV_ROWS, c), lambda i: (i, 0)),
                  pl.BlockSpec((HIST_ROWS, c), fix),
                  pl.BlockSpec((1, c), fix), pl.BlockSpec((1, c), fix), pl.BlockSpec((1, c), fix)],
        out_specs=pl.BlockSpec((CONV_ROWS, c), lambda i: (i, 0)),
        out_shape=jax.ShapeDtypeStruct((m, c), BF16),
        scratch_shapes=[pltpu.VMEM((HIST_ROWS + CONV_ROWS, c), F32),
                        pltpu.VMEM((CONV_ROWS, c), F32)],
        compiler_params=_params(("parallel",)),
        name="conv",
    )(hist, u, w_pad, b_dw.reshape(1, c), g.reshape(1, c), b.reshape(1, c))


def _ln_body(x_ref, g_ref, b_ref, *o_refs):
    x = x_ref[...]
    mu = jnp.mean(x, axis=-1, keepdims=True)
    xc = x - mu
    var = jnp.mean(xc * xc, axis=-1, keepdims=True)
    y = xc * lax.rsqrt(var + LN_EPS) * g_ref[...] + b_ref[...]
    for o in o_refs:
        o[...] = y.astype(o.dtype)


def _layer_norm(x, g, b, dtypes, *, tm=256):
    m, d = x.shape
    tm = _tile(m, tm)
    row = lambda i: (i, 0)
    fix = lambda i: (0, 0)
    return pl.pallas_call(
        _ln_body,
        grid=(m // tm,),
        in_specs=[pl.BlockSpec((tm, d), row), pl.BlockSpec((1, d), fix), pl.BlockSpec((1, d), fix)],
        out_specs=[pl.BlockSpec((tm, d), row) for _ in dtypes],
        out_shape=[jax.ShapeDtypeStruct((m, d), dt) for dt in dtypes],
        compiler_params=_params(("parallel",)),
        name="layer_norm",
    )(x, g.reshape(1, d), b.reshape(1, d))


def _row_reduce(x, combine, reduce):
    part = x[:, :LANES]
    for c in range(LANES, x.shape[1], LANES):
        part = combine(part, x[:, c:c + LANES])
    return reduce(part, axis=1, keepdims=True)


def _attn_prompt_body(q_ref, kn_ref, kr_ref, v_ref, o_ref, kf_ref, vt_ref, qt_ref, s0_ref, s1_ref,
                      p0_ref, p1_ref, acc0_ref, acc1_ref, *, tq):
    seq = q_ref.shape[0]
    kf_ref[:, :NOPE_DIM] = kn_ref[...]
    kf_ref[:, NOPE_DIM:] = kr_ref[...]
    for t in range(seq // tq):
        vt_ref[t, :V_DIM, :] = v_ref[t * tq:(t + 1) * tq, :].T
        vt_ref[t, V_DIM:, :] = jnp.ones((ONES_ROWS, tq), BF16)
        qt_ref[t] = q_ref[t * tq:(t + 1) * tq, :].T
    key_chunk = lax.broadcasted_iota(jnp.int32, (tq, tq), 0) // CHUNK
    q_chunk = lax.broadcasted_iota(jnp.int32, (tq, tq), 1) // CHUNK
    visible = key_chunk <= q_chunk

    def scores(qi, kj):
        return jnp.dot(kf_ref[kj * tq:(kj + 1) * tq, :], qt_ref[qi], preferred_element_type=F32)

    def over_keys(x, combine, reduce):
        part = x[0:tq // SUBLANES]
        for g in range(1, SUBLANES):
            part = combine(part, x[g * tq // SUBLANES:(g + 1) * tq // SUBLANES])
        return reduce(part, axis=0, keepdims=True)

    s_refs, p_refs, acc_refs = (s0_ref, s1_ref), (p0_ref, p1_ref), (acc0_ref, acc1_ref)
    step_no = 0
    for qi in range(seq // tq):
        acc_ref = acc_refs[qi % 2]
        m = jnp.full((1, tq), NEG_INF, F32)
        acc = a_prev = None
        s_refs[step_no % 2][...] = scores(qi, 0)
        for j in range(qi + 1):
            cur, nxt = step_no % 2, (step_no + 1) % 2
            last = j == qi
            if not last:
                s_refs[nxt][...] = scores(qi, j + 1)
            if j > 0:
                pv_prev = jnp.dot(vt_ref[j - 1], p_refs[nxt][...], preferred_element_type=F32)
                acc = pv_prev if j == 1 else acc_ref[...] * a_prev + pv_prev
            s = s_refs[cur][...]
            if last:
                s = jnp.where(visible, s, NEG_INF)
            m_new = jnp.maximum(m, over_keys(s, jnp.maximum, jnp.max))
            alpha = jnp.exp2(m - m_new)
            p = jnp.exp2(s - m_new)
            m = m_new
            p_refs[cur][...] = p.astype(BF16)
            if last:
                pv_last = jnp.dot(vt_ref[j], p_refs[cur][...], preferred_element_type=F32)
                acc = pv_last if acc is None else acc * alpha + pv_last
                out_t = acc[:V_DIM] / acc[V_DIM:V_DIM + 1]
                o_ref[qi * tq:(qi + 1) * tq, :] = out_t.T.astype(o_ref.dtype)
            else:
                if acc is not None:
                    acc_ref[...] = acc
                a_prev = alpha
            step_no += 1


def _attn_prompt(q, kv, kr, *, batch, seq, tq=512):
    tq = _tile(seq, tq)
    assert tq % CHUNK == 0
    per_head = (NOPE_DIM + V_DIM) // LANES
    body = functools.partial(_attn_prompt_body, tq=tq)
    return pl.pallas_call(
        body,
        grid=(batch, N_HEADS),
        in_specs=[pl.BlockSpec((seq, HEAD_SLOT), lambda b, h: (b, h)),
                  pl.BlockSpec((seq, NOPE_DIM), lambda b, h: (b, per_head * h)),
                  pl.BlockSpec((seq, ROPE_SLOT), lambda b, h: (b, 0)),
                  pl.BlockSpec((seq, V_DIM), lambda b, h: (b, per_head * h + 1))],
        out_specs=pl.BlockSpec((seq, V_DIM), lambda b, h: (b, h)),
        out_shape=jax.ShapeDtypeStruct((batch * seq, N_HEADS * V_DIM), BF16),
        scratch_shapes=[pltpu.VMEM((seq, HEAD_SLOT), BF16),
                        pltpu.VMEM((seq // tq, V_DIM + ONES_ROWS, tq), BF16),
                        pltpu.VMEM((seq // tq, HEAD_SLOT, tq), BF16),
                        pltpu.VMEM((tq, tq), F32), pltpu.VMEM((tq, tq), F32),
                        pltpu.VMEM((tq, tq), BF16), pltpu.VMEM((tq, tq), BF16),
                        pltpu.VMEM((V_DIM + ONES_ROWS, tq), F32),
                        pltpu.VMEM((V_DIM + ONES_ROWS, tq), F32)],
        compiler_params=_params(("parallel", "parallel")),
        name="attn_prompt",
    )(q, kv, kr, kv)


def _absorb_q_body(qn_ref, qr_ref, wuk_ref, qa_ref, qro_ref, *, batch, t):
    qa = lax.dot_general(qn_ref[...], wuk_ref[...].astype(BF16), (((1,), (1,)), ((), ())),
                         preferred_element_type=F32).astype(BF16)
    for b in range(batch):
        qa_ref[b] = qa[b * t:(b + 1) * t]
        qro_ref[b] = qr_ref[b * t:(b + 1) * t, :]


def _absorb_q(q, w_kv, *, batch, t):
    m = batch * t
    kv_rank = w_kv.shape[0]
    per_head = (NOPE_DIM + V_DIM) // LANES
    q_per_head = HEAD_SLOT // LANES
    body = functools.partial(_absorb_q_body, batch=batch, t=t)
    return pl.pallas_call(
        body,
        grid=(N_HEADS,),
        in_specs=[pl.BlockSpec((m, NOPE_DIM), lambda h: (0, q_per_head * h)),
                  pl.BlockSpec((m, ROPE_SLOT), lambda h: (0, q_per_head * h + 1)),
                  pl.BlockSpec((kv_rank, NOPE_DIM), lambda h: (0, per_head * h))],
        out_specs=[pl.BlockSpec((batch, t, kv_rank), lambda h: (0, h, 0)),
                   pl.BlockSpec((batch, t, ROPE_SLOT), lambda h: (0, h, 0))],
        out_shape=[jax.ShapeDtypeStruct((batch, N_HEADS * t, kv_rank), BF16),
                   jax.ShapeDtypeStruct((batch, N_HEADS * t, ROPE_SLOT), BF16)],
        compiler_params=_params(("parallel",)),
        name="absorb_q",
    )(q, q, w_kv)


def _attn_sample_body(qa_ref, qr_ref, cc_ref, ck_ref, nc_ref, nk_ref, o_ref):
    qa, qr = qa_ref[0], qr_ref[0]
    cc, ck, nc, nk = cc_ref[0], ck_ref[0], nc_ref[0], nk_ref[0]
    dims = (((1,), (1,)), ((), ()))
    s_old = (lax.dot_general(qa, cc, dims, preferred_element_type=F32)
             + lax.dot_general(qr, ck, dims, preferred_element_type=F32))
    s_new = (lax.dot_general(qa, nc, dims, preferred_element_type=F32)
             + lax.dot_general(qr, nk, dims, preferred_element_type=F32))
    m = jnp.maximum(_row_reduce(s_old, jnp.maximum, jnp.max),
                    jnp.max(s_new, axis=1, keepdims=True))
    p_old = jnp.exp2(s_old - m)
    p_new = jnp.exp2(s_new - m)
    l = _row_reduce(p_old, jnp.add, jnp.sum) + jnp.sum(p_new, axis=1, keepdims=True)
    o = (jnp.dot(p_old.astype(BF16), cc, preferred_element_type=F32)
         + jnp.dot(p_new.astype(BF16), nc, preferred_element_type=F32))
    o_ref[0] = (o / l).astype(BF16)


def _attn_sample(qa, qr, cache_c, cache_k, new_c, new_k, *, tr=512):
    batch, rows, kv_rank = qa.shape
    past, t = cache_c.shape[1], new_c.shape[1]
    tr = _tile(rows, tr)
    return pl.pallas_call(
        _attn_sample_body,
        grid=(batch, rows // tr),
        in_specs=[pl.BlockSpec((1, tr, kv_rank), lambda b, r: (b, r, 0)),
                  pl.BlockSpec((1, tr, ROPE_SLOT), lambda b, r: (b, r, 0)),
                  pl.BlockSpec((1, past, kv_rank), lambda b, r: (b, 0, 0)),
                  pl.BlockSpec((1, past, ROPE_SLOT), lambda b, r: (b, 0, 0)),
                  pl.BlockSpec((1, t, kv_rank), lambda b, r: (b, 0, 0)),
                  pl.BlockSpec((1, t, ROPE_SLOT), lambda b, r: (b, 0, 0))],
        out_specs=pl.BlockSpec((1, tr, kv_rank), lambda b, r: (b, r, 0)),
        out_shape=jax.ShapeDtypeStruct((batch, rows, kv_rank), BF16),
        compiler_params=_params(("parallel", "parallel")),
        name="attn_sample",
    )(qa, qr, cache_c, cache_k, new_c, new_k)


def _unabsorb_body(o_ref, wuv_ref, out_ref):
    b, t, r = o_ref.shape
    out_ref[...] = jnp.dot(o_ref[...].reshape(b * t, r), wuv_ref[...].astype(BF16),
                           preferred_element_type=F32).astype(BF16)


def _unabsorb(o_lat, w_kv, *, t):
    batch, _, kv_rank = o_lat.shape
    per_head = (NOPE_DIM + V_DIM) // LANES
    return pl.pallas_call(
        _unabsorb_body,
        grid=(N_HEADS,),
        in_specs=[pl.BlockSpec((batch, t, kv_rank), lambda h: (0, h, 0)),
                  pl.BlockSpec((kv_rank, V_DIM), lambda h: (0, per_head * h + 1))],
        out_specs=pl.BlockSpec((batch * t, V_DIM), lambda h: (0, h)),
        out_shape=jax.ShapeDtypeStruct((batch * t, N_HEADS * V_DIM), BF16),
        compiler_params=_params(("parallel",)),
        name="unabsorb",
    )(o_lat, w_kv)


def _rope_tables(pos):
    half = ROPE_DIM // 2
    inv = ROPE_THETA ** (-jnp.arange(half, dtype=F32) / half)
    ang = pos.astype(F32)[:, None] * inv[None, :]
    cos, sin = jnp.cos(ang), jnp.sin(ang)
    zero = jnp.zeros_like(cos)
    cos_t = jnp.concatenate([cos, cos, zero, zero], axis=1)
    sin_lo = jnp.concatenate([-sin, zero, zero, zero], axis=1)
    sin_hi = jnp.concatenate([zero, sin, zero, zero], axis=1)
    return cos_t, sin_lo, sin_hi


def _prep_weights(l, w_in, b_in, w_q_b, w_kv_b, d, conv_dim, q_rank, kv_rank):
    w, b = jnp.swapaxes(w_in[l], 0, 1), b_in[l]
    o_q = 2 * conv_dim
    o_kv = o_q + q_rank
    o_kr = o_kv + kv_rank
    o_g = o_kr + ROPE_DIM
    pad = ROPE_SLOT - ROPE_DIM
    w_lat = jnp.concatenate([w[o_q:o_g], jnp.zeros((pad, d), w.dtype)], axis=0).astype(BF16)
    b_lat = jnp.concatenate([b[o_q:o_g], jnp.zeros((pad,), b.dtype)])
    return dict(
        w_in=w, b_ga=b[:conv_dim], b_gb=b[conv_dim:o_q], w_lat=w_lat, b_lat=b_lat,
        w_gate=w[o_g:], b_gate=b[o_g:], w_q=w_q_b[l], w_kv=w_kv_b[l])


def _token_mixer(x, wts, p, *, alpha):
    residual = functools.partial(_ep_residual, alpha=alpha)
    d = x.shape[1]
    gconv = _matmul_f32w(p["conv_act"], [(wts["w_pw"], 0)], [(0, p["gates"])], [F32], _ep_gate,
                         n=d)[0]
    merged = _matmul(p["attn"], [wts["w_o"]], [(1, p["gates"]), (0, gconv)], [BF16], _ep_merge,
                     tm=1024, tn=256, tk=8192)[0]
    pre1 = _matmul_f32w(merged, [(wts["w_out"], 0)], [(0, x)], [F32], residual, n=d)[0]
    h, hb = _layer_norm(pre1, wts["ln1_g"], wts["ln1_b"], [F32, BF16])
    act = _matmul_f32w(hb, [(wts["w_up"], 0)], [], [BF16], _ep_relu2,
                       n=wts["w_up"].shape[1])[0]
    pre2 = _matmul(act, [wts["w_down"]], [(0, h)], [F32], residual, tk=2048)[0]
    return _layer_norm(pre2, wts["ln2_g"], wts["ln2_b"], [F32])[0]


def _in_stage(x, tabs, wts, hist, *, zero_period, scale):
    xb = x.astype(BF16)
    c = wts["b_ga"].shape[0]
    u = _matmul_f32w(xb, [(wts["w_in"], 0), (wts["w_in"], c)],
                     [_row(wts["b_ga"]), _row(wts["b_gb"])], [F32], _ep_glu, n=c, tn=256,
                     transposed=True)[0]
    gates = _matmul_f32w(xb, [(wts["w_gate"], 0)], [_row(wts["b_gate"])], [F32],
                         _ep_sigmoid_bias, n=wts["b_gate"].shape[0], transposed=True)[0]
    qn, ckv, ckvb, kr, krb = _latents(xb, wts["w_lat"], wts["b_lat"], wts["q_a_g"],
                                      wts["kv_a_g"], tabs)
    conv_act = _conv(u, u if zero_period else hist, wts["w_dw"], wts["b_dw"],
                     wts["conv_ln_g"], wts["conv_ln_b"], zero_period=zero_period)
    n_q = N_HEADS * HEAD_SLOT
    tn_q = _tile(n_q, 1024)
    q = _matmul_f32w(qn, [(wts["w_q"], 0)], [("rows", t) for t in tabs], [BF16],
                     functools.partial(_ep_q, scale=scale), n=n_q, tn=tn_q,
                     w_block=tn_q // HEAD_SLOT * QK_DIM, prep=_pad_q_weight)[0]
    return dict(u=u, gates=gates, ckv=ckv, ckvb=ckvb, kr=kr, krb=krb, conv_act=conv_act, q=q)


def kernel(x_prompt, x_sample, cache_ckv, cache_krope, state_conv, w_in, b_in, w_dw, b_dw,
           conv_ln_g, conv_ln_b, w_conv_pw, q_a_g, w_q_b, kv_a_g, w_kv_b, w_attn_o, w_out,
           ln1_g, ln1_b, w_up, w_down, ln2_g, ln2_b):
    depth = w_in.shape[0]
    bp, sp, d = x_prompt.shape
    bs, ts, _ = x_sample.shape
    past = cache_ckv.shape[2]
    conv_dim = w_dw.shape[2]
    q_rank, kv_rank = q_a_g.shape[1], kv_a_g.shape[1]
    alpha = (2 * depth) ** 0.25
    scale = QK_DIM ** -0.5 * math.log2(math.e)
    hist_len = CONV_WIDTH - 1
    assert ts == CONV_ROWS and sp % CONV_ROWS == 0

    tabs_p = [jnp.tile(t, (bp, 1)) for t in _rope_tables(jnp.arange(sp))]
    tabs_s = [jnp.tile(t, (bs, 1)) for t in _rope_tables(past + jnp.arange(ts))]

    hp = x_prompt.reshape(bp * sp, d)
    hs = x_sample.reshape(bs * ts, d)
    outs = [[] for _ in range(6)]
    for l in range(depth):
        wts = _prep_weights(l, w_in, b_in, w_q_b, w_kv_b, d, conv_dim, q_rank, kv_rank)
        wts.update(
            w_dw=w_dw[l], b_dw=b_dw[l], conv_ln_g=conv_ln_g[l], conv_ln_b=conv_ln_b[l],
            q_a_g=q_a_g[l], kv_a_g=kv_a_g[l], w_pw=w_conv_pw[l],
            w_o=w_attn_o[l].astype(BF16), w_out=w_out[l],
            ln1_g=ln1_g[l], ln1_b=ln1_b[l], w_up=w_up[l],
            w_down=w_down[l].astype(BF16), ln2_g=ln2_g[l], ln2_b=ln2_b[l])

        p = _in_stage(hp, tabs_p, wts, None, zero_period=sp // CONV_ROWS, scale=scale)
        kv = _matmul_f32w(p["ckvb"], [(wts["w_kv"], 0)], [], [BF16], _ep_cast,
                          n=wts["w_kv"].shape[1], tn=2048)[0]
        p["attn"] = _attn_prompt(p["q"], kv, p["krb"], batch=bp, seq=sp)
        hp_new = _token_mixer(hp, wts, p, alpha=alpha)
        outs[0].append(p["ckv"].reshape(bp, sp, kv_rank))
        outs[1].append(p["kr"].reshape(bp, sp, ROPE_DIM))
        outs[2].append(p["u"].reshape(bp, sp, conv_dim)[:, sp - hist_len:])

        hist = jnp.pad(state_conv[l], ((0, 0), (HIST_PAD, 0), (0, 0))).reshape(bs * HIST_ROWS, conv_dim)
        s = _in_stage(hs, tabs_s, wts, hist, zero_period=0, scale=scale)
        qa, qr = _absorb_q(s["q"], wts["w_kv"], batch=bs, t=ts)
        cache_k = jnp.pad(cache_krope[l], ((0, 0), (0, 0), (0, ROPE_SLOT - ROPE_DIM))).astype(BF16)
        o_lat = _attn_sample(qa, qr, cache_ckv[l].astype(BF16), cache_k,
                             s["ckvb"].reshape(bs, ts, kv_rank), s["krb"].reshape(bs, ts, ROPE_SLOT))
        s["attn"] = _unabsorb(o_lat, wts["w_kv"], t=ts)
        hs_new = _token_mixer(hs, wts, s, alpha=alpha)
        outs[3].append(s["ckv"].reshape(bs, ts, kv_rank))
        outs[4].append(s["kr"].reshape(bs, ts, ROPE_DIM))
        outs[5].append(s["u"].reshape(bs, ts, conv_dim)[:, ts - hist_len:])
        hp, hs = hp_new, hs_new

    return (hp.reshape(bp, sp, d), hs.reshape(bs, ts, d), jnp.stack(outs[0]), jnp.stack(outs[1]),
            jnp.stack(outs[2]), jnp.stack(outs[3]), jnp.stack(outs[4]), jnp.stack(outs[5]))
```

```python
import functools
import math

import jax
import jax.numpy as jnp
from jax import lax
from jax.experimental import pallas as pl
from jax.experimental.pallas import tpu as pltpu

F32 = jnp.float32
BF16 = jnp.bfloat16

CHUNK = 64
CONV_WIDTH = 31
N_HEADS = 64
NOPE_DIM = 128
ROPE_DIM = 64
V_DIM = 128
QK_DIM = NOPE_DIM + ROPE_DIM
ROPE_THETA = 10000.0
LN_EPS = 1e-5
RMS_EPS = 1e-6
NEG_INF = -1e30

LANES = 128
SUBLANES = 8
VMEM_LIMIT_BYTES = 60 * 1024 * 1024

ROPE_SLOT = LANES
HEAD_SLOT = NOPE_DIM + ROPE_SLOT
HIST_ROWS = 32
HIST_PAD = HIST_ROWS - (CONV_WIDTH - 1)
CONV_ROWS = 64
ONES_ROWS = 2 * SUBLANES


def _tile(dim, pref):
    t = min(dim, pref)
    assert dim % t == 0, (dim, pref)
    return t


def _params(sem):
    return pltpu.CompilerParams(dimension_semantics=sem, vmem_limit_bytes=VMEM_LIMIT_BYTES)


def _mm_body(*refs, nw, ne, no, nk, epilogue, transposed, acc_in_out):
    x_ref = refs[0]
    w_refs = refs[1:1 + nw]
    ex = refs[1 + nw:1 + nw + ne]
    outs = refs[1 + nw + ne:1 + nw + ne + no]
    accs = outs[:1] if acc_in_out else refs[1 + nw + ne + no:]
    dims = (((1,), (1 if transposed else 0,)), ((), ()))
    parts = [lax.dot_general(x_ref[...], w[...], dims, preferred_element_type=F32)
             for w in w_refs]
    if nk == 1:
        epilogue(parts, ex, outs)
        return
    k = pl.program_id(2)

    @pl.when(k == 0)
    def _():
        for a, p in zip(accs, parts):
            a[...] = p

    @pl.when(jnp.logical_and(k > 0, k < nk - 1))
    def _():
        for a, p in zip(accs, parts):
            a[...] += p

    @pl.when(k == nk - 1)
    def _():
        epilogue([a[...] + p for a, p in zip(accs, parts)], ex, outs)


def _extra_spec(kind, arr, tm, tn, n, ij):
    if kind == "row":
        return pl.BlockSpec((1, tn), lambda *g: (0, ij(*g)[1]))
    if kind == "rows":
        return pl.BlockSpec((tm, arr.shape[1]), lambda *g: (ij(*g)[0], 0))
    off = kind * (n // tn)
    return pl.BlockSpec((tm, tn), lambda *g: (ij(*g)[0], ij(*g)[1] + off))


def _matmul(x, ws, extras, outs, epilogue, *, tm=1024, tn=1024, tk=4096, transposed=False,
            acc_in_out=False):
    m, kdim = x.shape
    n = ws[0].shape[0 if transposed else 1]
    assert not acc_in_out or (len(ws) == 1 and outs[0] == F32)
    tm, tn, tk = _tile(m, tm), _tile(n, tn), _tile(kdim, tk)
    nk = kdim // tk
    in_specs = [pl.BlockSpec((tm, tk), lambda i, j, k: (i, k))]
    if transposed:
        in_specs += [pl.BlockSpec((tn, tk), lambda i, j, k: (j, k)) for _ in ws]
    else:
        in_specs += [pl.BlockSpec((tk, tn), lambda i, j, k: (k, j)) for _ in ws]
    in_specs += [_extra_spec(kind, arr, tm, tn, n, lambda i, j, k: (i, j)) for kind, arr in extras]
    out_specs = [pl.BlockSpec((tm, tn), lambda i, j, k: (i, j)) for _ in outs]
    out_shape = [jax.ShapeDtypeStruct((m, n), dt) for dt in outs]
    scratch = [pltpu.VMEM((tm, tn), F32) for _ in ws] if nk > 1 and not acc_in_out else []
    body = functools.partial(_mm_body, nw=len(ws), ne=len(extras), no=len(outs), nk=nk,
                             epilogue=epilogue, transposed=transposed, acc_in_out=acc_in_out)
    return pl.pallas_call(
        body,
        grid=(m // tm, n // tn, nk),
        in_specs=in_specs,
        out_specs=out_specs,
        out_shape=out_shape,
        scratch_shapes=scratch,
        compiler_params=_params(("parallel", "parallel", "arbitrary")),
        name="mm" + getattr(epilogue, "func", epilogue).__name__,
    )(x, *ws, *[arr for _, arr in extras])


def _row(vec):
    return ("row", vec.reshape(1, -1))


def _cast_weight(w_ref, wb_ref):
    wb_ref[...] = w_ref[...].astype(BF16)


def _pad_q_weight(w_ref, wb_ref):
    for h in range(wb_ref.shape[1] // HEAD_SLOT):
        c0 = h * HEAD_SLOT
        wb_ref[:, c0:c0 + QK_DIM] = w_ref[:, h * QK_DIM:(h + 1) * QK_DIM].astype(BF16)
        wb_ref[:, c0 + QK_DIM:c0 + HEAD_SLOT] = jnp.zeros(
            (wb_ref.shape[0], HEAD_SLOT - QK_DIM), BF16)


def _mmw_body(*refs, nw, ne, no, epilogue, prep, transposed):
    x_ref = refs[0]
    w_refs = refs[1:1 + nw]
    ex = refs[1 + nw:1 + nw + ne]
    outs = refs[1 + nw + ne:1 + nw + ne + no]
    wb_refs = refs[1 + nw + ne + no:]

    @pl.when(pl.program_id(1) == 0)
    def _():
        for w, wb in zip(w_refs, wb_refs):
            prep(w, wb)

    dims = (((1,), (1 if transposed else 0,)), ((), ()))
    parts = [lax.dot_general(x_ref[...], wb[...], dims, preferred_element_type=F32)
             for wb in wb_refs]
    epilogue(parts, ex, outs)


def _matmul_f32w(x, ws, extras, outs, epilogue, *, n, tm=1024, tn=512, w_block=None,
                 prep=_cast_weight, transposed=False):
    m, kdim = x.shape
    tm, tn = _tile(m, tm), _tile(n, tn)
    w_block = w_block or tn
    ij = lambda j, i: (i, j)
    in_specs = [pl.BlockSpec((tm, kdim), lambda j, i: (i, 0))]
    for _, col0 in ws:
        assert col0 % w_block == 0
        in_specs.append(pl.BlockSpec(
            (w_block, kdim) if transposed else (kdim, w_block),
            functools.partial(_w_index, col0 // w_block, transposed)))
    in_specs += [_extra_spec(kind, arr, tm, tn, n, ij) for kind, arr in extras]
    body = functools.partial(_mmw_body, nw=len(ws), ne=len(extras), no=len(outs),
                             epilogue=epilogue, prep=prep, transposed=transposed)
    return pl.pallas_call(
        body,
        grid=(n // tn, m // tm),
        in_specs=in_specs,
        out_specs=[pl.BlockSpec((tm, tn), lambda j, i: (i, j)) for _ in outs],
        out_shape=[jax.ShapeDtypeStruct((m, n), dt) for dt in outs],
        scratch_shapes=[pltpu.VMEM((tn, kdim) if transposed else (kdim, tn), BF16) for _ in ws],
        compiler_params=_params(("parallel", "arbitrary")),
        name="mmw" + getattr(epilogue, "func", epilogue).__name__,
    )(x, *[w for w, _ in ws], *[arr for _, arr in extras])


def _w_index(off, transposed, j, i):
    return (j + off, 0) if transposed else (0, j + off)


def _ep_glu(parts, ex, outs):
    a = parts[0] + ex[0][...]
    b = parts[1] + ex[1][...]
    outs[0][...] = a * jax.nn.sigmoid(b)


def _ep_sigmoid_bias(parts, ex, outs):
    outs[0][...] = jax.nn.sigmoid(parts[0] + ex[0][...])


def _rope_slot(a, cos, sin_lo, sin_hi):
    half = ROPE_DIM // 2
    return (a * cos + pltpu.roll(a, ROPE_SLOT - half, axis=1) * sin_lo
            + pltpu.roll(a, half, axis=1) * sin_hi)


def _ep_q(parts, ex, outs, *, scale):
    q = parts[0]
    cos, sin_lo, sin_hi = ex[0][...], ex[1][...], ex[2][...]
    for s in range(q.shape[1] // HEAD_SLOT):
        c0 = s * HEAD_SLOT
        outs[0][:, c0:c0 + NOPE_DIM] = (q[:, c0:c0 + NOPE_DIM] * scale).astype(BF16)
        r = _rope_slot(q[:, c0 + NOPE_DIM:c0 + HEAD_SLOT], cos, sin_lo, sin_hi)
        outs[0][:, c0 + NOPE_DIM:c0 + HEAD_SLOT] = (r * scale).astype(BF16)


def _ep_cast(parts, ex, outs):
    outs[0][...] = parts[0].astype(outs[0].dtype)


def _ep_gate(parts, ex, outs):
    outs[0][...] = ex[0][...] * parts[0]


def _ep_merge(parts, ex, outs):
    outs[0][...] = (ex[0][...] * parts[0] + ex[1][...]).astype(BF16)


def _ep_residual(parts, ex, outs, *, alpha):
    outs[0][...] = alpha * ex[0][...] + parts[0]


def _ep_relu2(parts, ex, outs):
    r = jnp.maximum(parts[0], 0.0)
    outs[0][...] = (r * r).astype(BF16)


def _rms(x, g):
    return x * lax.rsqrt(jnp.mean(x * x, axis=-1, keepdims=True) + RMS_EPS) * g


def _latent_body(x_ref, w_ref, b_ref, qg_ref, kvg_ref, cos_ref, slo_ref, shi_ref,
                 qn_ref, ckv_ref, ckvb_ref, kr_ref, krb_ref, *, q_rank, kv_rank):
    acc = lax.dot_general(x_ref[...], w_ref[...], (((1,), (1,)), ((), ())),
                          preferred_element_type=F32) + b_ref[...]
    qn_ref[...] = _rms(acc[:, :q_rank], qg_ref[...]).astype(BF16)
    ckv = _rms(acc[:, q_rank:q_rank + kv_rank], kvg_ref[...])
    ckv_ref[...] = ckv
    ckvb_ref[...] = ckv.astype(BF16)
    kr = _rope_slot(acc[:, q_rank + kv_rank:], cos_ref[...], slo_ref[...], shi_ref[...])
    kr_ref[...] = kr[:, :ROPE_DIM]
    krb_ref[...] = kr.astype(BF16)


def _latents(xb, w_lat, b_lat, q_g, kv_g, tabs, *, tm=512):
    m, d = xb.shape
    q_rank, kv_rank = q_g.shape[0], kv_g.shape[0]
    n = w_lat.shape[0]
    tm = _tile(m, tm)
    row = lambda i: (i, 0)
    fix = lambda i: (0, 0)
    body = functools.partial(_latent_body, q_rank=q_rank, kv_rank=kv_rank)
    return pl.pallas_call(
        body,
        grid=(m // tm,),
        in_specs=[pl.BlockSpec((tm, d), row), pl.BlockSpec((n, d), fix),
                  pl.BlockSpec((1, n), fix), pl.BlockSpec((1, q_rank), fix),
                  pl.BlockSpec((1, kv_rank), fix)]
                 + [pl.BlockSpec((tm, ROPE_SLOT), row)] * 3,
        out_specs=[pl.BlockSpec((tm, q_rank), row), pl.BlockSpec((tm, kv_rank), row),
                   pl.BlockSpec((tm, kv_rank), row), pl.BlockSpec((tm, ROPE_DIM), row),
                   pl.BlockSpec((tm, ROPE_SLOT), row)],
        out_shape=[jax.ShapeDtypeStruct((m, q_rank), BF16),
                   jax.ShapeDtypeStruct((m, kv_rank), F32),
                   jax.ShapeDtypeStruct((m, kv_rank), BF16),
                   jax.ShapeDtypeStruct((m, ROPE_DIM), F32),
                   jax.ShapeDtypeStruct((m, ROPE_SLOT), BF16)],
        compiler_params=_params(("parallel",)),
        name="latents",
    )(xb, w_lat, b_lat.reshape(1, n), q_g.reshape(1, -1), kv_g.reshape(1, -1), *tabs)


def _conv_body(hist_ref, u_ref, w_ref, bdw_ref, g_ref, b_ref, o_ref, win_ref, y_ref,
               *, zero_period, lane_chunk):
    c = u_ref.shape[1]
    hist = hist_ref[...]
    if zero_period:
        first = (pl.program_id(0) % zero_period) == 0
        hist = jnp.where(first, 0.0, hist)
    win_ref[0:HIST_ROWS, :] = hist
    win_ref[HIST_ROWS:, :] = u_ref[...]
    for c0 in range(0, c, lane_chunk):
        lanes = slice(c0, c0 + lane_chunk)
        acc = bdw_ref[:, lanes]
        for r in range(SUBLANES):
            rows = CONV_ROWS if r == 0 else CONV_ROWS + SUBLANES
            z = None
            for a in range((HIST_PAD + CONV_WIDTH - 1 - r) // SUBLANES + 1):
                k = a * SUBLANES + r - HIST_PAD
                if k < 0:
                    continue
                term = win_ref[a * SUBLANES:a * SUBLANES + rows, lanes] * w_ref[k:k + 1, lanes]
                z = term if z is None else z + term
            acc = acc + z[r:r + CONV_ROWS]
        y_ref[:, lanes] = acc
    y = y_ref[...]
    mu = jnp.mean(y, axis=-1, keepdims=True)
    yc = y - mu
    var = jnp.mean(yc * yc, axis=-1, keepdims=True)
    z = yc * lax.rsqrt(var + LN_EPS) * g_ref[...] + b_ref[...]
    o_ref[...] = (z * jax.nn.sigmoid(z)).astype(BF16)


def _conv(u, hist, w_dw, b_dw, g, b, *, zero_period):
    m, c = u.shape
    w_pad = jnp.pad(w_dw, ((0, HIST_ROWS - CONV_WIDTH), (0, 0)))
    per = CONV_ROWS // HIST_ROWS
    if zero_period:
        hist_map = lambda i: (jnp.maximum(i * per - 1, 0), 0)
    else:
        hist_map = lambda i: (i, 0)
    fix = lambda i: (0, 0)
    body = functools.partial(_conv_body, zero_period=zero_period, lane_chunk=min(c, 2 * LANES))
    return pl.pallas_call(
        body,
        grid=(m // CONV_ROWS,),
        in_specs=[pl.BlockSpec((HIST_ROWS, c), hist_map),
                  pl.BlockSpec((CONV_ROWS, c), lambda i: (i, 0)),
                  pl.BlockSpec((HIST_ROWS, c), fix),
                  pl.BlockSpec((1, c), fix), pl.BlockSpec((1, c), fix), pl.BlockSpec((1, c), fix)],
        out_specs=pl.BlockSpec((CONV_ROWS, c), lambda i: (i, 0)),
        out_shape=jax.ShapeDtypeStruct((m, c), BF16),
        scratch_shapes=[pltpu.VMEM((HIST_ROWS + CONV_ROWS, c), F32),
                        pltpu.VMEM((CONV_ROWS, c), F32)],
        compiler_params=_params(("parallel",)),
        name="conv",
    )(hist, u, w_pad, b_dw.reshape(1, c), g.reshape(1, c), b.reshape(1, c))


def _ln_body(x_ref, g_ref, b_ref, *o_refs):
    x = x_ref[...]
    mu = jnp.mean(x, axis=-1, keepdims=True)
    xc = x - mu
    var = jnp.mean(xc * xc, axis=-1, keepdims=True)
    y = xc * lax.rsqrt(var + LN_EPS) * g_ref[...] + b_ref[...]
    for o in o_refs:
        o[...] = y.astype(o.dtype)


def _layer_norm(x, g, b, dtypes, *, tm=512):
    m, d = x.shape
    tm = _tile(m, tm)
    row = lambda i: (i, 0)
    fix = lambda i: (0, 0)
    return pl.pallas_call(
        _ln_body,
        grid=(m // tm,),
        in_specs=[pl.BlockSpec((tm, d), row), pl.BlockSpec((1, d), fix), pl.BlockSpec((1, d), fix)],
        out_specs=[pl.BlockSpec((tm, d), row) for _ in dtypes],
        out_shape=[jax.ShapeDtypeStruct((m, d), dt) for dt in dtypes],
        compiler_params=_params(("parallel",)),
        name="layer_norm",
    )(x, g.reshape(1, d), b.reshape(1, d))


def _row_reduce(x, combine, reduce):
    part = x[:, :LANES]
    for c in range(LANES, x.shape[1], LANES):
        part = combine(part, x[:, c:c + LANES])
    return reduce(part, axis=1, keepdims=True)


def _attn_prompt_body(q_ref, kn_ref, kr_ref, v_ref, o_ref, kf_ref, vt_ref, qt_ref, s0_ref, s1_ref,
                      p0_ref, p1_ref, acc0_ref, acc1_ref, *, tq):
    seq = q_ref.shape[0]
    kf_ref[:, :NOPE_DIM] = kn_ref[...]
    kf_ref[:, NOPE_DIM:] = kr_ref[...]
    for t in range(seq // tq):
        vt_ref[t, :V_DIM, :] = v_ref[t * tq:(t + 1) * tq, :].T
        vt_ref[t, V_DIM:, :] = jnp.ones((ONES_ROWS, tq), BF16)
        qt_ref[t] = q_ref[t * tq:(t + 1) * tq, :].T
    key_chunk = lax.broadcasted_iota(jnp.int32, (tq, tq), 0) // CHUNK
    q_chunk = lax.broadcasted_iota(jnp.int32, (tq, tq), 1) // CHUNK
    visible = key_chunk <= q_chunk

    def scores(qi, kj):
        return jnp.dot(kf_ref[kj * tq:(kj + 1) * tq, :], qt_ref[qi], preferred_element_type=F32)

    def over_keys(x, combine, reduce):
        part = x[0:tq // SUBLANES]
        for g in range(1, SUBLANES):
            part = combine(part, x[g * tq // SUBLANES:(g + 1) * tq // SUBLANES])
        return reduce(part, axis=0, keepdims=True)

    s_refs, p_refs, acc_refs = (s0_ref, s1_ref), (p0_ref, p1_ref), (acc0_ref, acc1_ref)
    step_no = 0
    for qi in range(seq // tq):
        acc_ref = acc_refs[qi % 2]
        m = jnp.full((1, tq), NEG_INF, F32)
        acc = a_prev = None
        s_refs[step_no % 2][...] = scores(qi, 0)
        for j in range(qi + 1):
            cur, nxt = step_no % 2, (step_no + 1) % 2
            last = j == qi
            if not last:
                s_refs[nxt][...] = scores(qi, j + 1)
            if j > 0:
                pv_prev = jnp.dot(vt_ref[j - 1], p_refs[nxt][...], preferred_element_type=F32)
                acc = pv_prev if j == 1 else acc_ref[...] * a_prev + pv_prev
            s = s_refs[cur][...]
            if last:
                s = jnp.where(visible, s, NEG_INF)
            m_new = jnp.maximum(m, over_keys(s, jnp.maximum, jnp.max))
            alpha = jnp.exp2(m - m_new)
            p = jnp.exp2(s - m_new)
            m = m_new
            p_refs[cur][...] = p.astype(BF16)
            if last:
                pv_last = jnp.dot(vt_ref[j], p_refs[cur][...], preferred_element_type=F32)
                acc = pv_last if acc is None else acc * alpha + pv_last
                out_t = acc[:V_DIM] / acc[V_DIM:V_DIM + 1]
                o_ref[qi * tq:(qi + 1) * tq, :] = out_t.T.astype(o_ref.dtype)
            else:
                if acc is not None:
                    acc_ref[...] = acc
                a_prev = alpha
            step_no += 1


def _attn_prompt(q, kv, kr, *, batch, seq, tq=512):
    tq = _tile(seq, tq)
    assert tq % CHUNK == 0
    per_head = (NOPE_DIM + V_DIM) // LANES
    body = functools.partial(_attn_prompt_body, tq=tq)
    return pl.pallas_call(
        body,
        grid=(batch, N_HEADS),
        in_specs=[pl.BlockSpec((seq, HEAD_SLOT), lambda b, h: (b, h)),
                  pl.BlockSpec((seq, NOPE_DIM), lambda b, h: (b, per_head * h)),
                  pl.BlockSpec((seq, ROPE_SLOT), lambda b, h: (b, 0)),
                  pl.BlockSpec((seq, V_DIM), lambda b, h: (b, per_head * h + 1))],
        out_specs=pl.BlockSpec((seq, V_DIM), lambda b, h: (b, h)),
        out_shape=jax.ShapeDtypeStruct((batch * seq, N_HEADS * V_DIM), BF16),
        scratch_shapes=[pltpu.VMEM((seq, HEAD_SLOT), BF16),
                        pltpu.VMEM((seq // tq, V_DIM + ONES_ROWS, tq), BF16),
                        pltpu.VMEM((seq // tq, HEAD_SLOT, tq), BF16),
                        pltpu.VMEM((tq, tq), F32), pltpu.VMEM((tq, tq), F32),
                        pltpu.VMEM((tq, tq), BF16), pltpu.VMEM((tq, tq), BF16),
                        pltpu.VMEM((V_DIM + ONES_ROWS, tq), F32),
                        pltpu.VMEM((V_DIM + ONES_ROWS, tq), F32)],
        compiler_params=_params(("parallel", "parallel")),
        name="attn_prompt",
    )(q, kv, kr, kv)


def _absorb_q_body(qn_ref, qr_ref, wuk_ref, qa_ref, qro_ref, *, batch, t):
    qa = lax.dot_general(qn_ref[...], wuk_ref[...].astype(BF16), (((1,), (1,)), ((), ())),
                         preferred_element_type=F32).astype(BF16)
    for b in range(batch):
        qa_ref[b] = qa[b * t:(b + 1) * t]
        qro_ref[b] = qr_ref[b * t:(b + 1) * t, :]


def _absorb_q(q, w_kv, *, batch, t):
    m = batch * t
    kv_rank = w_kv.shape[0]
    per_head = (NOPE_DIM + V_DIM) // LANES
    q_per_head = HEAD_SLOT // LANES
    body = functools.partial(_absorb_q_body, batch=batch, t=t)
    return pl.pallas_call(
        body,
        grid=(N_HEADS,),
        in_specs=[pl.BlockSpec((m, NOPE_DIM), lambda h: (0, q_per_head * h)),
                  pl.BlockSpec((m, ROPE_SLOT), lambda h: (0, q_per_head * h + 1)),
                  pl.BlockSpec((kv_rank, NOPE_DIM), lambda h: (0, per_head * h))],
        out_specs=[pl.BlockSpec((batch, t, kv_rank), lambda h: (0, h, 0)),
                   pl.BlockSpec((batch, t, ROPE_SLOT), lambda h: (0, h, 0))],
        out_shape=[jax.ShapeDtypeStruct((batch, N_HEADS * t, kv_rank), BF16),
                   jax.ShapeDtypeStruct((batch, N_HEADS * t, ROPE_SLOT), BF16)],
        compiler_params=_params(("parallel",)),
        name="absorb_q",
    )(q, q, w_kv)


def _attn_sample_body(qa_ref, qr_ref, cc_ref, ck_ref, nc_ref, nk_ref, o_ref):
    qa, qr = qa_ref[0], qr_ref[0]
    cc, ck, nc, nk = cc_ref[0], ck_ref[0], nc_ref[0], nk_ref[0]
    dims = (((1,), (1,)), ((), ()))
    s_old = (lax.dot_general(qa, cc, dims, preferred_element_type=F32)
             + lax.dot_general(qr, ck, dims, preferred_element_type=F32))
    s_new = (lax.dot_general(qa, nc, dims, preferred_element_type=F32)
             + lax.dot_general(qr, nk, dims, preferred_element_type=F32))
    m = jnp.maximum(_row_reduce(s_old, jnp.maximum, jnp.max),
                    jnp.max(s_new, axis=1, keepdims=True))
    p_old = jnp.exp2(s_old - m)
    p_new = jnp.exp2(s_new - m)
    l = _row_reduce(p_old, jnp.add, jnp.sum) + jnp.sum(p_new, axis=1, keepdims=True)
    o = (jnp.dot(p_old.astype(BF16), cc, preferred_element_type=F32)
         + jnp.dot(p_new.astype(BF16), nc, preferred_element_type=F32))
    o_ref[0] = (o / l).astype(BF16)


def _attn_sample(qa, qr, cache_c, cache_k, new_c, new_k, *, tr=512):
    batch, rows, kv_rank = qa.shape
    past, t = cache_c.shape[1], new_c.shape[1]
    tr = _tile(rows, tr)
    return pl.pallas_call(
        _attn_sample_body,
        grid=(batch, rows // tr),
        in_specs=[pl.BlockSpec((1, tr, kv_rank), lambda b, r: (b, r, 0)),
                  pl.BlockSpec((1, tr, ROPE_SLOT), lambda b, r: (b, r, 0)),
                  pl.BlockSpec((1, past, kv_rank), lambda b, r: (b, 0, 0)),
                  pl.BlockSpec((1, past, ROPE_SLOT), lambda b, r: (b, 0, 0)),
                  pl.BlockSpec((1, t, kv_rank), lambda b, r: (b, 0, 0)),
                  pl.BlockSpec((1, t, ROPE_SLOT), lambda b, r: (b, 0, 0))],
        out_specs=pl.BlockSpec((1, tr, kv_rank), lambda b, r: (b, r, 0)),
        out_shape=jax.ShapeDtypeStruct((batch, rows, kv_rank), BF16),
        compiler_params=_params(("parallel", "parallel")),
        name="attn_sample",
    )(qa, qr, cache_c, cache_k, new_c, new_k)


def _unabsorb_body(o_ref, wuv_ref, out_ref):
    b, t, r = o_ref.shape
    out_ref[...] = jnp.dot(o_ref[...].reshape(b * t, r), wuv_ref[...].astype(BF16),
                           preferred_element_type=F32).astype(BF16)


def _unabsorb(o_lat, w_kv, *, t):
    batch, _, kv_rank = o_lat.shape
    per_head = (NOPE_DIM + V_DIM) // LANES
    return pl.pallas_call(
        _unabsorb_body,
        grid=(N_HEADS,),
        in_specs=[pl.BlockSpec((batch, t, kv_rank), lambda h: (0, h, 0)),
                  pl.BlockSpec((kv_rank, V_DIM), lambda h: (0, per_head * h + 1))],
        out_specs=pl.BlockSpec((batch * t, V_DIM), lambda h: (0, h)),
        out_shape=jax.ShapeDtypeStruct((batch * t, N_HEADS * V_DIM), BF16),
        compiler_params=_params(("parallel",)),
        name="unabsorb",
    )(o_lat, w_kv)


def _rope_tables(pos):
    half = ROPE_DIM // 2
    inv = ROPE_THETA ** (-jnp.arange(half, dtype=F32) / half)
    ang = pos.astype(F32)[:, None] * inv[None, :]
    cos, sin = jnp.cos(ang), jnp.sin(ang)
    zero = jnp.zeros_like(cos)
    cos_t = jnp.concatenate([cos, cos, zero, zero], axis=1)
    sin_lo = jnp.concatenate([-sin, zero, zero, zero], axis=1)
    sin_hi = jnp.concatenate([zero, sin, zero, zero], axis=1)
    return cos_t, sin_lo, sin_hi


def _prep_weights(l, w_in, b_in, w_q_b, w_kv_b, d, conv_dim, q_rank, kv_rank):
    w, b = jnp.swapaxes(w_in[l], 0, 1), b_in[l]
    o_q = 2 * conv_dim
    o_kv = o_q + q_rank
    o_kr = o_kv + kv_rank
    o_g = o_kr + ROPE_DIM
    pad = ROPE_SLOT - ROPE_DIM
    w_lat = lax.optimization_barrier(w[o_q:o_g])
    w_lat = jnp.concatenate([w_lat, jnp.zeros((pad, d), w.dtype)], axis=0).astype(BF16)
    b_lat = jnp.concatenate([b[o_q:o_g], jnp.zeros((pad,), b.dtype)])
    return dict(
        w_in=w, b_ga=b[:conv_dim], b_gb=b[conv_dim:o_q], w_lat=w_lat, b_lat=b_lat,
        w_gate=w[o_g:].astype(BF16), b_gate=b[o_g:], w_q=w_q_b[l], w_kv=w_kv_b[l])


def _token_mixer(x, wts, p, *, alpha):
    residual = functools.partial(_ep_residual, alpha=alpha)
    d = x.shape[1]
    gconv = _matmul_f32w(p["conv_act"], [(wts["w_pw"], 0)], [(0, p["gates"])], [F32], _ep_gate,
                         n=d)[0]
    merged = _matmul(p["attn"], [wts["w_o"]], [(1, p["gates"]), (0, gconv)], [BF16], _ep_merge,
                     tm=1024, tn=256, tk=8192)[0]
    pre1 = _matmul_f32w(merged, [(wts["w_out"], 0)], [(0, x)], [F32], residual, n=d)[0]
    h, hb = _layer_norm(pre1, wts["ln1_g"], wts["ln1_b"], [F32, BF16])
    act = _matmul_f32w(hb, [(wts["w_up"], 0)], [], [BF16], _ep_relu2,
                       n=wts["w_up"].shape[1])[0]
    pre2 = _matmul(act, [wts["w_down"]], [(0, h)], [F32], residual, acc_in_out=True)[0]
    return _layer_norm(pre2, wts["ln2_g"], wts["ln2_b"], [F32])[0]


def _in_stage(x, tabs, wts, hist, *, zero_period, scale):
    xb = x.astype(BF16)
    c = wts["b_ga"].shape[0]
    u = _matmul_f32w(xb, [(wts["w_in"], 0), (wts["w_in"], c)],
                     [_row(wts["b_ga"]), _row(wts["b_gb"])], [F32], _ep_glu, n=c, tn=256,
                     transposed=True)[0]
    gates = _matmul(xb, [wts["w_gate"]], [_row(wts["b_gate"])], [F32], _ep_sigmoid_bias,
                    transposed=True)[0]
    qn, ckv, ckvb, kr, krb = _latents(xb, wts["w_lat"], wts["b_lat"], wts["q_a_g"],
                                      wts["kv_a_g"], tabs)
    conv_act = _conv(u, u if zero_period else hist, wts["w_dw"], wts["b_dw"],
                     wts["conv_ln_g"], wts["conv_ln_b"], zero_period=zero_period)
    n_q = N_HEADS * HEAD_SLOT
    tn_q = _tile(n_q, 1024)
    q = _matmul_f32w(qn, [(wts["w_q"], 0)], [("rows", t) for t in tabs], [BF16],
                     functools.partial(_ep_q, scale=scale), n=n_q, tn=tn_q,
                     w_block=tn_q // HEAD_SLOT * QK_DIM, prep=_pad_q_weight)[0]
    return dict(u=u, gates=gates, ckv=ckv, ckvb=ckvb, kr=kr, krb=krb, conv_act=conv_act, q=q)


def kernel(x_prompt, x_sample, cache_ckv, cache_krope, state_conv, w_in, b_in, w_dw, b_dw,
           conv_ln_g, conv_ln_b, w_conv_pw, q_a_g, w_q_b, kv_a_g, w_kv_b, w_attn_o, w_out,
           ln1_g, ln1_b, w_up, w_down, ln2_g, ln2_b):
    depth = w_in.shape[0]
    bp, sp, d = x_prompt.shape
    bs, ts, _ = x_sample.shape
    past = cache_ckv.shape[2]
    conv_dim = w_dw.shape[2]
    q_rank, kv_rank = q_a_g.shape[1], kv_a_g.shape[1]
    alpha = (2 * depth) ** 0.25
    scale = QK_DIM ** -0.5 * math.log2(math.e)
    hist_len = CONV_WIDTH - 1
    assert ts == CONV_ROWS and sp % CONV_ROWS == 0

    tabs_p = [jnp.tile(t, (bp, 1)) for t in _rope_tables(jnp.arange(sp))]
    tabs_s = [jnp.tile(t, (bs, 1)) for t in _rope_tables(past + jnp.arange(ts))]

    hp = x_prompt.reshape(bp * sp, d)
    hs = x_sample.reshape(bs * ts, d)
    outs = [[] for _ in range(6)]
    for l in range(depth):
        wts = _prep_weights(l, w_in, b_in, w_q_b, w_kv_b, d, conv_dim, q_rank, kv_rank)
        wts.update(
            w_dw=w_dw[l], b_dw=b_dw[l], conv_ln_g=conv_ln_g[l], conv_ln_b=conv_ln_b[l],
            q_a_g=q_a_g[l], kv_a_g=kv_a_g[l], w_pw=w_conv_pw[l],
            w_o=w_attn_o[l].astype(BF16), w_out=w_out[l],
            ln1_g=ln1_g[l], ln1_b=ln1_b[l], w_up=w_up[l],
            w_down=w_down[l].astype(BF16), ln2_g=ln2_g[l], ln2_b=ln2_b[l])

        p = _in_stage(hp, tabs_p, wts, None, zero_period=sp // CONV_ROWS, scale=scale)
        kv = _matmul_f32w(p["ckvb"], [(wts["w_kv"], 0)], [], [BF16], _ep_cast,
                          n=wts["w_kv"].shape[1], tn=2048)[0]
        p["attn"] = _attn_prompt(p["q"], kv, p["krb"], batch=bp, seq=sp)
        hp_new = _token_mixer(hp, wts, p, alpha=alpha)
        outs[0].append(p["ckv"].reshape(bp, sp, kv_rank))
        outs[1].append(p["kr"].reshape(bp, sp, ROPE_DIM))
        outs[2].append(p["u"].reshape(bp, sp, conv_dim)[:, sp - hist_len:])

        hist = jnp.pad(state_conv[l], ((0, 0), (HIST_PAD, 0), (0, 0))).reshape(bs * HIST_ROWS, conv_dim)
        s = _in_stage(hs, tabs_s, wts, hist, zero_period=0, scale=scale)
        qa, qr = _absorb_q(s["q"], wts["w_kv"], batch=bs, t=ts)
        cache_k = jnp.pad(cache_krope[l], ((0, 0), (0, 0), (0, ROPE_SLOT - ROPE_DIM))).astype(BF16)
        o_lat = _attn_sample(qa, qr, cache_ckv[l].astype(BF16), cache_k,
                             s["ckvb"].reshape(bs, ts, kv_rank), s["krb"].reshape(bs, ts, ROPE_SLOT))
        s["attn"] = _unabsorb(o_lat, wts["w_kv"], t=ts)
        hs_new = _token_mixer(hs, wts, s, alpha=alpha)
        outs[3].append(s["ckv"].reshape(bs, ts, kv_rank))
        outs[4].append(s["kr"].reshape(bs, ts, ROPE_DIM))
        outs[5].append(s["u"].reshape(bs, ts, conv_dim)[:, ts - hist_len:])
        hp, hs = hp_new, hs_new

    return (hp.reshape(bp, sp, d), hs.reshape(bs, ts, d), jnp.stack(outs[0]), jnp.stack(outs[1]),
            jnp.stack(outs[2]), jnp.stack(outs[3]), jnp.stack(outs[4]), jnp.stack(outs[5]))
```

```python
import functools
import math

import jax
import jax.numpy as jnp
from jax import lax
from jax.experimental import pallas as pl
from jax.experimental.pallas import tpu as pltpu

F32 = jnp.float32
BF16 = jnp.bfloat16

CHUNK = 64
CONV_WIDTH = 31
N_HEADS = 64
NOPE_DIM = 128
ROPE_DIM = 64
V_DIM = 128
QK_DIM = NOPE_DIM + ROPE_DIM
ROPE_THETA = 10000.0
LN_EPS = 1e-5
RMS_EPS = 1e-6
NEG_INF = -1e30

LANES = 128
SUBLANES = 8
VMEM_LIMIT_BYTES = 60 * 1024 * 1024

ROPE_SLOT = LANES
HEAD_SLOT = NOPE_DIM + ROPE_SLOT
HIST_ROWS = 32
HIST_PAD = HIST_ROWS - (CONV_WIDTH - 1)
CONV_ROWS = 64
ONES_ROWS = 2 * SUBLANES


def _tile(dim, pref):
    t = min(dim, pref)
    assert dim % t == 0, (dim, pref)
    return t


def _params(sem):
    return pltpu.CompilerParams(dimension_semantics=sem, vmem_limit_bytes=VMEM_LIMIT_BYTES)


def _mm_body(*refs, nw, ne, no, nk, epilogue, transposed, acc_in_out):
    x_ref = refs[0]
    w_refs = refs[1:1 + nw]
    ex = refs[1 + nw:1 + nw + ne]
    outs = refs[1 + nw + ne:1 + nw + ne + no]
    accs = outs[:1] if acc_in_out else refs[1 + nw + ne + no:]
    dims = (((1,), (1 if transposed else 0,)), ((), ()))
    parts = [lax.dot_general(x_ref[...], w[...], dims, preferred_element_type=F32)
             for w in w_refs]
    if nk == 1:
        epilogue(parts, ex, outs)
        return
    k = pl.program_id(2)

    @pl.when(k == 0)
    def _():
        for a, p in zip(accs, parts):
            a[...] = p

    @pl.when(jnp.logical_and(k > 0, k < nk - 1))
    def _():
        for a, p in zip(accs, parts):
            a[...] += p

    @pl.when(k == nk - 1)
    def _():
        epilogue([a[...] + p for a, p in zip(accs, parts)], ex, outs)


def _extra_spec(kind, arr, tm, tn, n, ij):
    if kind == "row":
        return pl.BlockSpec((1, tn), lambda *g: (0, ij(*g)[1]))
    if kind == "rows":
        return pl.BlockSpec((tm, arr.shape[1]), lambda *g: (ij(*g)[0], 0))
    off = kind * (n // tn)
    return pl.BlockSpec((tm, tn), lambda *g: (ij(*g)[0], ij(*g)[1] + off))


def _matmul(x, ws, extras, outs, epilogue, *, tm=1024, tn=1024, tk=4096, transposed=False,
            acc_in_out=False):
    m, kdim = x.shape
    n = ws[0].shape[0 if transposed else 1]
    assert not acc_in_out or (len(ws) == 1 and outs[0] == F32)
    tm, tn, tk = _tile(m, tm), _tile(n, tn), _tile(kdim, tk)
    nk = kdim // tk
    in_specs = [pl.BlockSpec((tm, tk), lambda i, j, k: (i, k))]
    if transposed:
        in_specs += [pl.BlockSpec((tn, tk), lambda i, j, k: (j, k)) for _ in ws]
    else:
        in_specs += [pl.BlockSpec((tk, tn), lambda i, j, k: (k, j)) for _ in ws]
    in_specs += [_extra_spec(kind, arr, tm, tn, n, lambda i, j, k: (i, j)) for kind, arr in extras]
    out_specs = [pl.BlockSpec((tm, tn), lambda i, j, k: (i, j)) for _ in outs]
    out_shape = [jax.ShapeDtypeStruct((m, n), dt) for dt in outs]
    scratch = [pltpu.VMEM((tm, tn), F32) for _ in ws] if nk > 1 and not acc_in_out else []
    body = functools.partial(_mm_body, nw=len(ws), ne=len(extras), no=len(outs), nk=nk,
                             epilogue=epilogue, transposed=transposed, acc_in_out=acc_in_out)
    return pl.pallas_call(
        body,
        grid=(m // tm, n // tn, nk),
        in_specs=in_specs,
        out_specs=out_specs,
        out_shape=out_shape,
        scratch_shapes=scratch,
        compiler_params=_params(("parallel", "parallel", "arbitrary")),
        name="mm" + getattr(epilogue, "func", epilogue).__name__,
    )(x, *ws, *[arr for _, arr in extras])


def _row(vec):
    return ("row", vec.reshape(1, -1))


def _cast_weight(w_ref, wb_ref):
    wb_ref[...] = w_ref[...].astype(BF16)


def _pad_q_weight(w_ref, wb_ref):
    for h in range(wb_ref.shape[1] // HEAD_SLOT):
        c0 = h * HEAD_SLOT
        wb_ref[:, c0:c0 + QK_DIM] = w_ref[:, h * QK_DIM:(h + 1) * QK_DIM].astype(BF16)
        wb_ref[:, c0 + QK_DIM:c0 + HEAD_SLOT] = jnp.zeros(
            (wb_ref.shape[0], HEAD_SLOT - QK_DIM), BF16)


def _mmw_body(*refs, nw, ne, no, epilogue, prep, transposed):
    x_ref = refs[0]
    w_refs = refs[1:1 + nw]
    ex = refs[1 + nw:1 + nw + ne]
    outs = refs[1 + nw + ne:1 + nw + ne + no]
    wb_refs = refs[1 + nw + ne + no:]

    @pl.when(pl.program_id(1) == 0)
    def _():
        for w, wb in zip(w_refs, wb_refs):
            prep(w, wb)

    dims = (((1,), (1 if transposed else 0,)), ((), ()))
    parts = [lax.dot_general(x_ref[...], wb[...], dims, preferred_element_type=F32)
             for wb in wb_refs]
    epilogue(parts, ex, outs)


def _matmul_f32w(x, ws, extras, outs, epilogue, *, n, tm=1024, tn=512, w_block=None,
                 prep=_cast_weight, transposed=False):
    m, kdim = x.shape
    tm, tn = _tile(m, tm), _tile(n, tn)
    w_block = w_block or tn
    ij = lambda j, i: (i, j)
    in_specs = [pl.BlockSpec((tm, kdim), lambda j, i: (i, 0))]
    for _, col0 in ws:
        assert col0 % w_block == 0
        in_specs.append(pl.BlockSpec(
            (w_block, kdim) if transposed else (kdim, w_block),
            functools.partial(_w_index, col0 // w_block, transposed)))
    in_specs += [_extra_spec(kind, arr, tm, tn, n, ij) for kind, arr in extras]
    body = functools.partial(_mmw_body, nw=len(ws), ne=len(extras), no=len(outs),
                             epilogue=epilogue, prep=prep, transposed=transposed)
    return pl.pallas_call(
        body,
        grid=(n // tn, m // tm),
        in_specs=in_specs,
        out_specs=[pl.BlockSpec((tm, tn), lambda j, i: (i, j)) for _ in outs],
        out_shape=[jax.ShapeDtypeStruct((m, n), dt) for dt in outs],
        scratch_shapes=[pltpu.VMEM((tn, kdim) if transposed else (kdim, tn), BF16) for _ in ws],
        compiler_params=_params(("parallel", "arbitrary")),
        name="mmw" + getattr(epilogue, "func", epilogue).__name__,
    )(x, *[w for w, _ in ws], *[arr for _, arr in extras])


def _w_index(off, transposed, j, i):
    return (j + off, 0) if transposed else (0, j + off)


def _ep_glu(parts, ex, outs):
    a = parts[0] + ex[0][...]
    b = parts[1] + ex[1][...]
    outs[0][...] = a * jax.nn.sigmoid(b)


def _ep_sigmoid_bias(parts, ex, outs):
    outs[0][...] = jax.nn.sigmoid(parts[0] + ex[0][...])


def _rope_slot(a, cos, sin_lo, sin_hi):
    half = ROPE_DIM // 2
    return (a * cos + pltpu.roll(a, ROPE_SLOT - half, axis=1) * sin_lo
            + pltpu.roll(a, half, axis=1) * sin_hi)


def _ep_q(parts, ex, outs, *, scale):
    q = parts[0]
    cos, sin_lo, sin_hi = ex[0][...], ex[1][...], ex[2][...]
    for s in range(q.shape[1] // HEAD_SLOT):
        c0 = s * HEAD_SLOT
        outs[0][:, c0:c0 + NOPE_DIM] = (q[:, c0:c0 + NOPE_DIM] * scale).astype(BF16)
        r = _rope_slot(q[:, c0 + NOPE_DIM:c0 + HEAD_SLOT], cos, sin_lo, sin_hi)
        outs[0][:, c0 + NOPE_DIM:c0 + HEAD_SLOT] = (r * scale).astype(BF16)


def _ep_cast(parts, ex, outs):
    outs[0][...] = parts[0].astype(outs[0].dtype)


def _ep_gate(parts, ex, outs):
    outs[0][...] = ex[0][...] * parts[0]


def _ep_merge(parts, ex, outs):
    outs[0][...] = (ex[0][...] * parts[0] + ex[1][...]).astype(BF16)


def _ep_residual(parts, ex, outs, *, alpha):
    outs[0][...] = alpha * ex[0][...] + parts[0]


def _ep_relu2(parts, ex, outs):
    r = jnp.maximum(parts[0], 0.0)
    outs[0][...] = (r * r).astype(BF16)


def _rms(x, g):
    return x * lax.rsqrt(jnp.mean(x * x, axis=-1, keepdims=True) + RMS_EPS) * g


def _latent_body(x_ref, w_ref, b_ref, qg_ref, kvg_ref, cos_ref, slo_ref, shi_ref,
                 qn_ref, ckv_ref, ckvb_ref, kr_ref, krb_ref, *, q_rank, kv_rank):
    acc = lax.dot_general(x_ref[...], w_ref[...], (((1,), (1,)), ((), ())),
                          preferred_element_type=F32) + b_ref[...]
    qn_ref[...] = _rms(acc[:, :q_rank], qg_ref[...]).astype(BF16)
    ckv = _rms(acc[:, q_rank:q_rank + kv_rank], kvg_ref[...])
    ckv_ref[...] = ckv
    ckvb_ref[...] = ckv.astype(BF16)
    kr = _rope_slot(acc[:, q_rank + kv_rank:], cos_ref[...], slo_ref[...], shi_ref[...])
    kr_ref[...] = kr[:, :ROPE_DIM]
    krb_ref[...] = kr.astype(BF16)


def _latents(xb, w_lat, b_lat, q_g, kv_g, tabs, *, tm=512):
    m, d = xb.shape
    q_rank, kv_rank = q_g.shape[0], kv_g.shape[0]
    n = w_lat.shape[0]
    tm = _tile(m, tm)
    row = lambda i: (i, 0)
    fix = lambda i: (0, 0)
    body = functools.partial(_latent_body, q_rank=q_rank, kv_rank=kv_rank)
    return pl.pallas_call(
        body,
        grid=(m // tm,),
        in_specs=[pl.BlockSpec((tm, d), row), pl.BlockSpec((n, d), fix),
                  pl.BlockSpec((1, n), fix), pl.BlockSpec((1, q_rank), fix),
                  pl.BlockSpec((1, kv_rank), fix)]
                 + [pl.BlockSpec((tm, ROPE_SLOT), row)] * 3,
        out_specs=[pl.BlockSpec((tm, q_rank), row), pl.BlockSpec((tm, kv_rank), row),
                   pl.BlockSpec((tm, kv_rank), row), pl.BlockSpec((tm, ROPE_DIM), row),
                   pl.BlockSpec((tm, ROPE_SLOT), row)],
        out_shape=[jax.ShapeDtypeStruct((m, q_rank), BF16),
                   jax.ShapeDtypeStruct((m, kv_rank), F32),
                   jax.ShapeDtypeStruct((m, kv_rank), BF16),
                   jax.ShapeDtypeStruct((m, ROPE_DIM), F32),
                   jax.ShapeDtypeStruct((m, ROPE_SLOT), BF16)],
        compiler_params=_params(("parallel",)),
        name="latents",
    )(xb, w_lat, b_lat.reshape(1, n), q_g.reshape(1, -1), kv_g.reshape(1, -1), *tabs)


def _conv_body(hist_ref, u_ref, w_ref, bdw_ref, g_ref, b_ref, o_ref, win_ref, y_ref,
               *, zero_period, lane_chunk):
    c = u_ref.shape[1]
    hist = hist_ref[...]
    if zero_period:
        first = (pl.program_id(0) % zero_period) == 0
        hist = jnp.where(first, 0.0, hist)
    win_ref[0:HIST_ROWS, :] = hist
    win_ref[HIST_ROWS:, :] = u_ref[...]
    for c0 in range(0, c, lane_chunk):
        lanes = slice(c0, c0 + lane_chunk)
        acc = bdw_ref[:, lanes]
        for r in range(SUBLANES):
            rows = CONV_ROWS if r == 0 else CONV_ROWS + SUBLANES
            z = None
            for a in range((HIST_PAD + CONV_WIDTH - 1 - r) // SUBLANES + 1):
                k = a * SUBLANES + r - HIST_PAD
                if k < 0:
                    continue
                term = win_ref[a * SUBLANES:a * SUBLANES + rows, lanes] * w_ref[k:k + 1, lanes]
                z = term if z is None else z + term
            acc = acc + z[r:r + CONV_ROWS]
        y_ref[:, lanes] = acc
    y = y_ref[...]
    mu = jnp.mean(y, axis=-1, keepdims=True)
    yc = y - mu
    var = jnp.mean(yc * yc, axis=-1, keepdims=True)
    z = yc * lax.rsqrt(var + LN_EPS) * g_ref[...] + b_ref[...]
    o_ref[...] = (z * jax.nn.sigmoid(z)).astype(BF16)


def _conv(u, hist, w_dw, b_dw, g, b, *, zero_period):
    m, c = u.shape
    w_pad = jnp.pad(w_dw, ((0, HIST_ROWS - CONV_WIDTH), (0, 0)))
    per = CONV_ROWS // HIST_ROWS
    if zero_period:
        hist_map = lambda i: (jnp.maximum(i * per - 1, 0), 0)
    else:
        hist_map = lambda i: (i, 0)
    fix = lambda i: (0, 0)
    body = functools.partial(_conv_body, zero_period=zero_period, lane_chunk=min(c, LANES))
    return pl.pallas_call(
        body,
        grid=(m // CONV_ROWS,),
        in_specs=[pl.BlockSpec((HIST_ROWS, c), hist_map),
                  pl.BlockSpec((CONV_ROWS, c), lambda i: (i, 0)),
                  pl.BlockSpec((HIST_ROWS, c), fix),
                  pl.BlockSpec((1, c), fix), pl.BlockSpec((1, c), fix), pl.BlockSpec((1, c), fix)],
        out_specs=pl.BlockSpec((CONV_ROWS, c), lambda i: (i, 0)),
        out_shape=jax.ShapeDtypeStruct((m, c), BF16),
        scratch_shapes=[pltpu.VMEM((HIST_ROWS + CONV_ROWS, c), F32),
                        pltpu.VMEM((CONV_ROWS, c), F32)],
        compiler_params=_params(("parallel",)),
        name="conv",
    )(hist, u, w_pad, b_dw.reshape(1, c), g.reshape(1, c), b.reshape(1, c))


def _ln_body(x_ref, g_ref, b_ref, *o_refs):
    x = x_ref[...]
    mu = jnp.mean(x, axis=-1, keepdims=True)
    xc = x - mu
    var = jnp.mean(xc * xc, axis=-1, keepdims=True)
    y = xc * lax.rsqrt(var + LN_EPS) * g_ref[...] + b_ref[...]
    for o in o_refs:
        o[...] = y.astype(o.dtype)


def _layer_norm(x, g, b, dtypes, *, tm=512):
    m, d = x.shape
    tm = _tile(m, tm)
    row = lambda i: (i, 0)
    fix = lambda i: (0, 0)
    return pl.pallas_call(
        _ln_body,
        grid=(m // tm,),
        in_specs=[pl.BlockSpec((tm, d), row), pl.BlockSpec((1, d), fix), pl.BlockSpec((1, d), fix)],
        out_specs=[pl.BlockSpec((tm, d), row) for _ in dtypes],
        out_shape=[jax.ShapeDtypeStruct((m, d), dt) for dt in dtypes],
        compiler_params=_params(("parallel",)),
        name="layer_norm",
    )(x, g.reshape(1, d), b.reshape(1, d))


def _row_reduce(x, combine, reduce):
    part = x[:, :LANES]
    for c in range(LANES, x.shape[1], LANES):
        part = combine(part, x[:, c:c + LANES])
    return reduce(part, axis=1, keepdims=True)


def _attn_prompt_body(q_ref, kn_ref, kr_ref, v_ref, o_ref, kf_ref, vt_ref, qt_ref, s0_ref, s1_ref,
                      p0_ref, p1_ref, acc0_ref, acc1_ref, *, tq):
    seq = q_ref.shape[0]
    kf_ref[:, :NOPE_DIM] = kn_ref[...]
    kf_ref[:, NOPE_DIM:] = kr_ref[...]
    for t in range(seq // tq):
        vt_ref[t, :V_DIM, :] = v_ref[t * tq:(t + 1) * tq, :].T
        vt_ref[t, V_DIM:, :] = jnp.ones((ONES_ROWS, tq), BF16)
        qt_ref[t] = q_ref[t * tq:(t + 1) * tq, :].T
    key_chunk = lax.broadcasted_iota(jnp.int32, (tq, tq), 0) // CHUNK
    q_chunk = lax.broadcasted_iota(jnp.int32, (tq, tq), 1) // CHUNK
    visible = key_chunk <= q_chunk

    def scores(qi, kj):
        return jnp.dot(kf_ref[kj * tq:(kj + 1) * tq, :], qt_ref[qi], preferred_element_type=F32)

    def over_keys(x, combine, reduce):
        part = x[0:tq // SUBLANES]
        for g in range(1, SUBLANES):
            part = combine(part, x[g * tq // SUBLANES:(g + 1) * tq // SUBLANES])
        return reduce(part, axis=0, keepdims=True)

    s_refs, p_refs, acc_refs = (s0_ref, s1_ref), (p0_ref, p1_ref), (acc0_ref, acc1_ref)
    step_no = 0
    for qi in range(seq // tq):
        acc_ref = acc_refs[qi % 2]
        m = jnp.full((1, tq), NEG_INF, F32)
        acc = a_prev = None
        s_refs[step_no % 2][...] = scores(qi, 0)
        for j in range(qi + 1):
            cur, nxt = step_no % 2, (step_no + 1) % 2
            last = j == qi
            if not last:
                s_refs[nxt][...] = scores(qi, j + 1)
            if j > 0:
                pv_prev = jnp.dot(vt_ref[j - 1], p_refs[nxt][...], preferred_element_type=F32)
                acc = pv_prev if j == 1 else acc_ref[...] * a_prev + pv_prev
            s = s_refs[cur][...]
            if last:
                s = jnp.where(visible, s, NEG_INF)
            m_new = jnp.maximum(m, over_keys(s, jnp.maximum, jnp.max))
            alpha = jnp.exp2(m - m_new)
            p = jnp.exp2(s - m_new)
            m = m_new
            p_refs[cur][...] = p.astype(BF16)
            if last:
                pv_last = jnp.dot(vt_ref[j], p_refs[cur][...], preferred_element_type=F32)
                acc = pv_last if acc is None else acc * alpha + pv_last
                out_t = acc[:V_DIM] / acc[V_DIM:V_DIM + 1]
                o_ref[qi * tq:(qi + 1) * tq, :] = out_t.T.astype(o_ref.dtype)
            else:
                if acc is not None:
                    acc_ref[...] = acc
                a_prev = alpha
            step_no += 1


def _attn_prompt(q, kv, kr, *, batch, seq, tq=512):
    tq = _tile(seq, tq)
    assert tq % CHUNK == 0
    per_head = (NOPE_DIM + V_DIM) // LANES
    body = functools.partial(_attn_prompt_body, tq=tq)
    return pl.pallas_call(
        body,
        grid=(batch, N_HEADS),
        in_specs=[pl.BlockSpec((seq, HEAD_SLOT), lambda b, h: (b, h)),
                  pl.BlockSpec((seq, NOPE_DIM), lambda b, h: (b, per_head * h)),
                  pl.BlockSpec((seq, ROPE_SLOT), lambda b, h: (b, 0)),
                  pl.BlockSpec((seq, V_DIM), lambda b, h: (b, per_head * h + 1))],
        out_specs=pl.BlockSpec((seq, V_DIM), lambda b, h: (b, h)),
        out_shape=jax.ShapeDtypeStruct((batch * seq, N_HEADS * V_DIM), BF16),
        scratch_shapes=[pltpu.VMEM((seq, HEAD_SLOT), BF16),
                        pltpu.VMEM((seq // tq, V_DIM + ONES_ROWS, tq), BF16),
                        pltpu.VMEM((seq // tq, HEAD_SLOT, tq), BF16),
                        pltpu.VMEM((tq, tq), F32), pltpu.VMEM((tq, tq), F32),
                        pltpu.VMEM((tq, tq), BF16), pltpu.VMEM((tq, tq), BF16),
                        pltpu.VMEM((V_DIM + ONES_ROWS, tq), F32),
                        pltpu.VMEM((V_DIM + ONES_ROWS, tq), F32)],
        compiler_params=_params(("parallel", "parallel")),
        name="attn_prompt",
    )(q, kv, kr, kv)


def _absorb_q_body(qn_ref, qr_ref, wuk_ref, qa_ref, qro_ref, *, batch, t):
    qa = lax.dot_general(qn_ref[...], wuk_ref[...].astype(BF16), (((1,), (1,)), ((), ())),
                         preferred_element_type=F32).astype(BF16)
    for b in range(batch):
        qa_ref[b] = qa[b * t:(b + 1) * t]
        qro_ref[b] = qr_ref[b * t:(b + 1) * t, :]


def _absorb_q(q, w_kv, *, batch, t):
    m = batch * t
    kv_rank = w_kv.shape[0]
    per_head = (NOPE_DIM + V_DIM) // LANES
    q_per_head = HEAD_SLOT // LANES
    body = functools.partial(_absorb_q_body, batch=batch, t=t)
    return pl.pallas_call(
        body,
        grid=(N_HEADS,),
        in_specs=[pl.BlockSpec((m, NOPE_DIM), lambda h: (0, q_per_head * h)),
                  pl.BlockSpec((m, ROPE_SLOT), lambda h: (0, q_per_head * h + 1)),
                  pl.BlockSpec((kv_rank, NOPE_DIM), lambda h: (0, per_head * h))],
        out_specs=[pl.BlockSpec((batch, t, kv_rank), lambda h: (0, h, 0)),
                   pl.BlockSpec((batch, t, ROPE_SLOT), lambda h: (0, h, 0))],
        out_shape=[jax.ShapeDtypeStruct((batch, N_HEADS * t, kv_rank), BF16),
                   jax.ShapeDtypeStruct((batch, N_HEADS * t, ROPE_SLOT), BF16)],
        compiler_params=_params(("parallel",)),
        name="absorb_q",
    )(q, q, w_kv)


def _attn_sample_body(qa_ref, qr_ref, cc_ref, ck_ref, nc_ref, nk_ref, o_ref):
    qa, qr = qa_ref[0], qr_ref[0]
    cc, ck, nc, nk = cc_ref[0], ck_ref[0], nc_ref[0], nk_ref[0]
    dims = (((1,), (1,)), ((), ()))
    s_old = (lax.dot_general(qa, cc, dims, preferred_element_type=F32)
             + lax.dot_general(qr, ck, dims, preferred_element_type=F32))
    s_new = (lax.dot_general(qa, nc, dims, preferred_element_type=F32)
             + lax.dot_general(qr, nk, dims, preferred_element_type=F32))
    m = jnp.maximum(_row_reduce(s_old, jnp.maximum, jnp.max),
                    jnp.max(s_new, axis=1, keepdims=True))
    p_old = jnp.exp2(s_old - m)
    p_new = jnp.exp2(s_new - m)
    l = _row_reduce(p_old, jnp.add, jnp.sum) + jnp.sum(p_new, axis=1, keepdims=True)
    o = (jnp.dot(p_old.astype(BF16), cc, preferred_element_type=F32)
         + jnp.dot(p_new.astype(BF16), nc, preferred_element_type=F32))
    o_ref[0] = (o / l).astype(BF16)


def _attn_sample(qa, qr, cache_c, cache_k, new_c, new_k, *, tr=512):
    batch, rows, kv_rank = qa.shape
    past, t = cache_c.shape[1], new_c.shape[1]
    tr = _tile(rows, tr)
    return pl.pallas_call(
        _attn_sample_body,
        grid=(batch, rows // tr),
        in_specs=[pl.BlockSpec((1, tr, kv_rank), lambda b, r: (b, r, 0)),
                  pl.BlockSpec((1, tr, ROPE_SLOT), lambda b, r: (b, r, 0)),
                  pl.BlockSpec((1, past, kv_rank), lambda b, r: (b, 0, 0)),
                  pl.BlockSpec((1, past, ROPE_SLOT), lambda b, r: (b, 0, 0)),
                  pl.BlockSpec((1, t, kv_rank), lambda b, r: (b, 0, 0)),
                  pl.BlockSpec((1, t, ROPE_SLOT), lambda b, r: (b, 0, 0))],
        out_specs=pl.BlockSpec((1, tr, kv_rank), lambda b, r: (b, r, 0)),
        out_shape=jax.ShapeDtypeStruct((batch, rows, kv_rank), BF16),
        compiler_params=_params(("parallel", "parallel")),
        name="attn_sample",
    )(qa, qr, cache_c, cache_k, new_c, new_k)


def _unabsorb_body(o_ref, wuv_ref, out_ref):
    b, t, r = o_ref.shape
    out_ref[...] = jnp.dot(o_ref[...].reshape(b * t, r), wuv_ref[...].astype(BF16),
                           preferred_element_type=F32).astype(BF16)


def _unabsorb(o_lat, w_kv, *, t):
    batch, _, kv_rank = o_lat.shape
    per_head = (NOPE_DIM + V_DIM) // LANES
    return pl.pallas_call(
        _unabsorb_body,
        grid=(N_HEADS,),
        in_specs=[pl.BlockSpec((batch, t, kv_rank), lambda h: (0, h, 0)),
                  pl.BlockSpec((kv_rank, V_DIM), lambda h: (0, per_head * h + 1))],
        out_specs=pl.BlockSpec((batch * t, V_DIM), lambda h: (0, h)),
        out_shape=jax.ShapeDtypeStruct((batch * t, N_HEADS * V_DIM), BF16),
        compiler_params=_params(("parallel",)),
        name="unabsorb",
    )(o_lat, w_kv)


def _rope_tables(pos):
    half = ROPE_DIM // 2
    inv = ROPE_THETA ** (-jnp.arange(half, dtype=F32) / half)
    ang = pos.astype(F32)[:, None] * inv[None, :]
    cos, sin = jnp.cos(ang), jnp.sin(ang)
    zero = jnp.zeros_like(cos)
    cos_t = jnp.concatenate([cos, cos, zero, zero], axis=1)
    sin_lo = jnp.concatenate([-sin, zero, zero, zero], axis=1)
    sin_hi = jnp.concatenate([zero, sin, zero, zero], axis=1)
    return cos_t, sin_lo, sin_hi


def _prep_weights(l, w_in, b_in, w_q_b, w_kv_b, d, conv_dim, q_rank, kv_rank):
    w, b = jnp.swapaxes(w_in[l], 0, 1), b_in[l]
    o_q = 2 * conv_dim
    o_kv = o_q + q_rank
    o_kr = o_kv + kv_rank
    o_g = o_kr + ROPE_DIM
    pad = ROPE_SLOT - ROPE_DIM
    w_lat = lax.optimization_barrier(w[o_q:o_g])
    w_lat = jnp.concatenate([w_lat, jnp.zeros((pad, d), w.dtype)], axis=0).astype(BF16)
    b_lat = jnp.concatenate([b[o_q:o_g], jnp.zeros((pad,), b.dtype)])
    return dict(
        w_in=w, b_ga=b[:conv_dim], b_gb=b[conv_dim:o_q], w_lat=w_lat, b_lat=b_lat,
        w_gate=w[o_g:].astype(BF16), b_gate=b[o_g:], w_q=w_q_b[l], w_kv=w_kv_b[l])


def _token_mixer(x, wts, p, *, alpha):
    residual = functools.partial(_ep_residual, alpha=alpha)
    d = x.shape[1]
    gconv = _matmul_f32w(p["conv_act"], [(wts["w_pw"], 0)], [(0, p["gates"])], [F32], _ep_gate,
                         n=d)[0]
    merged = _matmul(p["attn"], [wts["w_o"]], [(1, p["gates"]), (0, gconv)], [BF16], _ep_merge,
                     tm=1024, tn=256, tk=8192)[0]
    pre1 = _matmul_f32w(merged, [(wts["w_out"], 0)], [(0, x)], [F32], residual, n=d)[0]
    h, hb = _layer_norm(pre1, wts["ln1_g"], wts["ln1_b"], [F32, BF16])
    act = _matmul_f32w(hb, [(wts["w_up"], 0)], [], [BF16], _ep_relu2,
                       n=wts["w_up"].shape[1])[0]
    pre2 = _matmul(act, [wts["w_down"]], [(0, h)], [F32], residual, acc_in_out=True)[0]
    return _layer_norm(pre2, wts["ln2_g"], wts["ln2_b"], [F32])[0]


def _in_stage(x, tabs, wts, hist, *, zero_period, scale):
    xb = x.astype(BF16)
    c = wts["b_ga"].shape[0]
    u = _matmul_f32w(xb, [(wts["w_in"], 0), (wts["w_in"], c)],
                     [_row(wts["b_ga"]), _row(wts["b_gb"])], [F32], _ep_glu, n=c, tn=256,
                     transposed=True)[0]
    gates = _matmul(xb, [wts["w_gate"]], [_row(wts["b_gate"])], [F32], _ep_sigmoid_bias,
                    transposed=True)[0]
    qn, ckv, ckvb, kr, krb = _latents(xb, wts["w_lat"], wts["b_lat"], wts["q_a_g"],
                                      wts["kv_a_g"], tabs)
    conv_act = _conv(u, u if zero_period else hist, wts["w_dw"], wts["b_dw"],
                     wts["conv_ln_g"], wts["conv_ln_b"], zero_period=zero_period)
    n_q = N_HEADS * HEAD_SLOT
    tn_q = _tile(n_q, 1024)
    q = _matmul_f32w(qn, [(wts["w_q"], 0)], [("rows", t) for t in tabs], [BF16],
                     functools.partial(_ep_q, scale=scale), n=n_q, tn=tn_q,
                     w_block=tn_q // HEAD_SLOT * QK_DIM, prep=_pad_q_weight)[0]
    return dict(u=u, gates=gates, ckv=ckv, ckvb=ckvb, kr=kr, krb=krb, conv_act=conv_act, q=q)


def kernel(x_prompt, x_sample, cache_ckv, cache_krope, state_conv, w_in, b_in, w_dw, b_dw,
           conv_ln_g, conv_ln_b, w_conv_pw, q_a_g, w_q_b, kv_a_g, w_kv_b, w_attn_o, w_out,
           ln1_g, ln1_b, w_up, w_down, ln2_g, ln2_b):
    depth = w_in.shape[0]
    bp, sp, d = x_prompt.shape
    bs, ts, _ = x_sample.shape
    past = cache_ckv.shape[2]
    conv_dim = w_dw.shape[2]
    q_rank, kv_rank = q_a_g.shape[1], kv_a_g.shape[1]
    alpha = (2 * depth) ** 0.25
    scale = QK_DIM ** -0.5 * math.log2(math.e)
    hist_len = CONV_WIDTH - 1
    assert ts == CONV_ROWS and sp % CONV_ROWS == 0

    tabs_p = [jnp.tile(t, (bp, 1)) for t in _rope_tables(jnp.arange(sp))]
    tabs_s = [jnp.tile(t, (bs, 1)) for t in _rope_tables(past + jnp.arange(ts))]

    hp = x_prompt.reshape(bp * sp, d)
    hs = x_sample.reshape(bs * ts, d)
    outs = [[] for _ in range(6)]
    for l in range(depth):
        wts = _prep_weights(l, w_in, b_in, w_q_b, w_kv_b, d, conv_dim, q_rank, kv_rank)
        wts.update(
            w_dw=w_dw[l], b_dw=b_dw[l], conv_ln_g=conv_ln_g[l], conv_ln_b=conv_ln_b[l],
            q_a_g=q_a_g[l], kv_a_g=kv_a_g[l], w_pw=w_conv_pw[l],
            w_o=w_attn_o[l].astype(BF16), w_out=w_out[l],
            ln1_g=ln1_g[l], ln1_b=ln1_b[l], w_up=w_up[l],
            w_down=w_down[l].astype(BF16), ln2_g=ln2_g[l], ln2_b=ln2_b[l])

        p = _in_stage(hp, tabs_p, wts, None, zero_period=sp // CONV_ROWS, scale=scale)
        kv = _matmul_f32w(p["ckvb"], [(wts["w_kv"], 0)], [], [BF16], _ep_cast,
                          n=wts["w_kv"].shape[1], tn=2048)[0]
        p["attn"] = _attn_prompt(p["q"], kv, p["krb"], batch=bp, seq=sp)
        hp_new = _token_mixer(hp, wts, p, alpha=alpha)
        outs[0].append(p["ckv"].reshape(bp, sp, kv_rank))
        outs[1].append(p["kr"].reshape(bp, sp, ROPE_DIM))
        outs[2].append(p["u"].reshape(bp, sp, conv_dim)[:, sp - hist_len:])

        hist = jnp.pad(state_conv[l], ((0, 0), (HIST_PAD, 0), (0, 0))).reshape(bs * HIST_ROWS, conv_dim)
        s = _in_stage(hs, tabs_s, wts, hist, zero_period=0, scale=scale)
        qa, qr = _absorb_q(s["q"], wts["w_kv"], batch=bs, t=ts)
        cache_k = jnp.pad(cache_krope[l], ((0, 0), (0, 0), (0, ROPE_SLOT - ROPE_DIM))).astype(BF16)
        o_lat = _attn_sample(qa, qr, cache_ckv[l].astype(BF16), cache_k,
                             s["ckvb"].reshape(bs, ts, kv_rank), s["krb"].reshape(bs, ts, ROPE_SLOT))
        s["attn"] = _unabsorb(o_lat, wts["w_kv"], t=ts)
        hs_new = _token_mixer(hs, wts, s, alpha=alpha)
        outs[3].append(s["ckv"].reshape(bs, ts, kv_rank))
        outs[4].append(s["kr"].reshape(bs, ts, ROPE_DIM))
        outs[5].append(s["u"].reshape(bs, ts, conv_dim)[:, ts - hist_len:])
        hp, hs = hp_new, hs_new

    return (hp.reshape(bp, sp, d), hs.reshape(bs, ts, d), jnp.stack(outs[0]), jnp.stack(outs[1]),
            jnp.stack(outs[2]), jnp.stack(outs[3]), jnp.stack(outs[4]), jnp.stack(outs[5]))
```
